```python
import jax, jax.numpy as jnp
from jax import lax
import numpy as np

D_MODEL = 2048
BATCH = 1
SEQ = 8192
DEPTH = 2

CHUNK = 64
N_MIXERS = 2
N_RET = (DEPTH + 1) // N_MIXERS
N_GDN = DEPTH // N_MIXERS

RET_HEADS = 8
RET_DK = D_MODEL // RET_HEADS
RET_DV = 2 * RET_DK
RET_QK = RET_HEADS * RET_DK
RET_VW = RET_HEADS * RET_DV
ROPE_BASE = 10000.0
GN_EPS = 1e-6

GDN_K_HEADS = 16
GDN_V_HEADS = 32
GDN_DK = D_MODEL // GDN_K_HEADS
GDN_DV = 2 * D_MODEL // GDN_V_HEADS
GDN_QK = GDN_K_HEADS * GDN_DK
GDN_VW = GDN_V_HEADS * GDN_DV
GDN_QKV = 2 * GDN_QK + GDN_VW
GDN_CONV = 4
RMS_EPS = 1e-6
L2_EPS = 1e-6

D_FF = 4 * D_MODEL

DN_ALPHA = (2.0 * DEPTH) ** 0.25
DN_BETA = (8.0 * DEPTH) ** -0.25
LN_EPS = 1e-5

kernel_name = 'chunk_causal_retention_gdn_hybrid'

F32 = jnp.float32


def layer_norm(x, g, b):
    xf = x.astype(F32)
    mu = jnp.mean(xf, -1, keepdims=True)
    var = jnp.mean(jnp.square(xf - mu), -1, keepdims=True)
    return ((xf - mu) * lax.rsqrt(var + LN_EPS) * g + b).astype(x.dtype)


def rotary(t, pos):
    half = t.shape[-1] // 2
    inv = ROPE_BASE ** (-jnp.arange(half, dtype=F32) / half)
    ang = pos.astype(F32)[:, None] * inv[None, :]
    cos = jnp.cos(ang)[None, :, None, :]
    sin = jnp.sin(ang)[None, :, None, :]
    t1, t2 = t[..., :half], t[..., half:]
    return jnp.concatenate([t1 * cos - t2 * sin, t1 * sin + t2 * cos], axis=-1)


def l2norm(t):
    return t * lax.rsqrt(jnp.sum(jnp.square(t), -1, keepdims=True) + L2_EPS)


def to_chunks(t):
    b, s, h, d = t.shape
    return t.reshape(b, s // CHUNK, CHUNK, h, d).transpose(0, 3, 1, 2, 4)


def to_chunks_scalar(t):
    b, s, h = t.shape
    return t.reshape(b, s // CHUNK, CHUNK, h).transpose(0, 3, 1, 2)


def from_chunks(t):
    b, h, n, c, d = t.shape
    return t.transpose(0, 2, 3, 1, 4).reshape(b, n * c, h, d)


def retention_mixer(x, w_in, gn_g, w_out):
    B, S, _ = x.shape
    q, k, v, gate = jnp.split(x @ w_in, [RET_QK, 2 * RET_QK, 2 * RET_QK + RET_VW], axis=-1)
    pos = jnp.arange(S)
    q = rotary(q.reshape(B, S, RET_HEADS, RET_DK).astype(F32), pos)
    k = rotary(k.reshape(B, S, RET_HEADS, RET_DK).astype(F32), pos) * (RET_DK ** -0.5)
    v = v.reshape(B, S, RET_HEADS, RET_DV).astype(F32)
    q, k, v = to_chunks(q), to_chunks(k), to_chunks(v)

    log_gamma = jnp.log1p(-jnp.exp2(-5.0 - jnp.arange(RET_HEADS, dtype=F32)))
    idx = jnp.arange(CHUNK, dtype=F32)
    lg = log_gamma[:, None]
    intra = jnp.exp(lg[..., None] * jnp.abs(idx[:, None] - idx[None, :]))
    scores = jnp.einsum('bhncd,bhnmd->bhncm', q, k) * intra[None, :, None]
    y_intra = jnp.einsum('bhncm,bhnme->bhnce', scores, v)

    q_dec = q * jnp.exp(lg * (idx + 1.0))[None, :, None, :, None]
    k_dec = k * jnp.exp(lg * (CHUNK - 1.0 - idx))[None, :, None, :, None]
    chunk_decay = jnp.exp(log_gamma * CHUNK)[None, :, None, None]

    def step(state, inp):
        qc, kc, vc = inp
        y = jnp.einsum('bhcd,bhde->bhce', qc, state)
        state = state * chunk_decay + jnp.einsum('bhcd,bhce->bhde', kc, vc)
        return state, y

    state0 = jnp.zeros((B, RET_HEADS, RET_DK, RET_DV), F32)
    _, y_inter = lax.scan(step, state0, (jnp.moveaxis(q_dec, 2, 0), jnp.moveaxis(k_dec, 2, 0), jnp.moveaxis(v, 2, 0)))
    y = from_chunks(y_intra + jnp.moveaxis(y_inter, 0, 2))

    mu = jnp.mean(y, -1, keepdims=True)
    var = jnp.mean(jnp.square(y - mu), -1, keepdims=True)
    y = ((y - mu) * lax.rsqrt(var + GN_EPS)).reshape(B, S, RET_VW) * gn_g
    return (jax.nn.silu(gate) * y.astype(x.dtype)) @ w_out


def gated_deltanet_mixer(x, w_in, conv_w, a_log, dt_bias, norm_g, w_out):
    B, S, _ = x.shape
    qkv, z, b, a = jnp.split(x @ w_in, [GDN_QKV, GDN_QKV + GDN_VW, GDN_QKV + GDN_VW + GDN_V_HEADS], axis=-1)
    qkv = jax.nn.silu(lax.conv_general_dilated(
        qkv, conv_w[:, None, :], window_strides=(1,), padding=[(GDN_CONV - 1, 0)],
        dimension_numbers=('NWC', 'WIO', 'NWC'), feature_group_count=GDN_QKV))
    q, k, v = jnp.split(qkv.astype(F32), [GDN_QK, 2 * GDN_QK], axis=-1)
    rep = GDN_V_HEADS // GDN_K_HEADS
    q = jnp.repeat(l2norm(q.reshape(B, S, GDN_K_HEADS, GDN_DK)) * (GDN_DK ** -0.5), rep, axis=2)
    k = jnp.repeat(l2norm(k.reshape(B, S, GDN_K_HEADS, GDN_DK)), rep, axis=2)
    v = v.reshape(B, S, GDN_V_HEADS, GDN_DV)
    beta = jax.nn.sigmoid(b.astype(F32))
    g = -jnp.exp(a_log.astype(F32)) * jax.nn.softplus(a.astype(F32) + dt_bias.astype(F32))

    q, k, v = to_chunks(q), to_chunks(k), to_chunks(v)
    beta, g = to_chunks_scalar(beta), to_chunks_scalar(g)
    g_cum = jnp.cumsum(g, axis=-1)
    idx = jnp.arange(CHUNK)
    causal = idx[:, None] >= idx[None, :]
    strict = idx[:, None] > idx[None, :]
    decay = jnp.exp(jnp.where(causal, g_cum[..., :, None] - g_cum[..., None, :], -jnp.inf))

    k_beta = k * beta[..., None]
    a_mat = jnp.where(strict, jnp.einsum('bhncd,bhnmd->bhncm', k_beta, k) * decay, 0.0)
    eye = jnp.eye(CHUNK, dtype=F32)
    t_mat = lax.linalg.triangular_solve(a_mat + eye, jnp.broadcast_to(eye, a_mat.shape),
                                        left_side=True, lower=True, unit_diagonal=True)
    u = jnp.einsum('bhncm,bhnme->bhnce', t_mat, v * beta[..., None])
    w = jnp.einsum('bhncm,bhnmd->bhncd', t_mat, k_beta * jnp.exp(g_cum)[..., None])
    attn = jnp.einsum('bhncd,bhnmd->bhncm', q, k) * decay
    q_dec = q * jnp.exp(g_cum)[..., None]
    k_dec = k * jnp.exp(g_cum[..., -1:] - g_cum)[..., None]
    chunk_decay = jnp.exp(g_cum[..., -1])

    def step(state, inp):
        qd, kd, wc, uc, ac, cd = inp
        v_new = uc - jnp.einsum('bhcd,bhde->bhce', wc, state)
        y = jnp.einsum('bhcd,bhde->bhce', qd, state) + jnp.einsum('bhcm,bhme->bhce', ac, v_new)
        state = state * cd[..., None, None] + jnp.einsum('bhcd,bhce->bhde', kd, v_new)
        return state, y

    xs = (jnp.moveaxis(q_dec, 2, 0), jnp.moveaxis(k_dec, 2, 0), jnp.moveaxis(w, 2, 0),
          jnp.moveaxis(u, 2, 0), jnp.moveaxis(attn, 2, 0), jnp.moveaxis(chunk_decay, 2, 0))
    state0 = jnp.zeros((B, GDN_V_HEADS, GDN_DK, GDN_DV), F32)
    _, y = lax.scan(step, state0, xs)
    y = from_chunks(jnp.moveaxis(y, 0, 2))

    y = y * lax.rsqrt(jnp.mean(jnp.square(y), -1, keepdims=True) + RMS_EPS) * norm_g
    y = y * jax.nn.silu(z.astype(F32).reshape(B, S, GDN_V_HEADS, GDN_DV))
    return y.reshape(B, S, GDN_VW).astype(x.dtype) @ w_out


def sq_relu_mlp(x, w1, w2):
    return jnp.square(jax.nn.relu(x @ w1)) @ w2


def setup_inputs(seed: int = 0) -> dict:
    key = jax.random.key(seed)
    ks = jax.random.split(key, 20)
    nrm = jax.random.normal
    x = nrm(ks[0], (BATCH, SEQ, D_MODEL), F32)

    ret_w_in = nrm(ks[1], (N_RET, D_MODEL, 2 * RET_QK + 2 * RET_VW), F32) * D_MODEL ** -0.5
    ret_gn_g = 1.0 + 0.02 * nrm(ks[2], (N_RET, RET_VW), F32)
    ret_w_out = nrm(ks[3], (N_RET, RET_VW, D_MODEL), F32) * (RET_VW ** -0.5 * DN_BETA)

    gdn_w_in = nrm(ks[4], (N_GDN, D_MODEL, GDN_QKV + GDN_VW + 2 * GDN_V_HEADS), F32) * D_MODEL ** -0.5
    gdn_conv_w = nrm(ks[5], (N_GDN, GDN_CONV, GDN_QKV), F32) * GDN_CONV ** -0.5
    gdn_a_log = jnp.log(jax.random.uniform(ks[6], (N_GDN, GDN_V_HEADS), F32, 1.0, 16.0))
    dt = jnp.exp(jax.random.uniform(ks[7], (N_GDN, GDN_V_HEADS), F32, np.log(1e-3), np.log(1e-1)))
    gdn_dt_bias = dt + jnp.log(-jnp.expm1(-dt))
    gdn_norm_g = 1.0 + 0.02 * nrm(ks[8], (N_GDN, GDN_DV), F32)
    gdn_w_out = nrm(ks[9], (N_GDN, GDN_VW, D_MODEL), F32) * (GDN_VW ** -0.5 * DN_BETA)

    ln_mix_g = 1.0 + 0.02 * nrm(ks[10], (DEPTH, D_MODEL), F32)
    ln_mix_b = 0.02 * nrm(ks[11], (DEPTH, D_MODEL), F32)
    mlp_w1 = nrm(ks[12], (DEPTH, D_MODEL, D_FF), F32) * D_MODEL ** -0.5
    mlp_w2 = nrm(ks[13], (DEPTH, D_FF, D_MODEL), F32) * (D_FF ** -0.5 * DN_BETA)
    ln_ffn_g = 1.0 + 0.02 * nrm(ks[14], (DEPTH, D_MODEL), F32)
    ln_ffn_b = 0.02 * nrm(ks[15], (DEPTH, D_MODEL), F32)
    return {'x': x,
            'ret_w_in': ret_w_in, 'ret_gn_g': ret_gn_g, 'ret_w_out': ret_w_out,
            'gdn_w_in': gdn_w_in, 'gdn_conv_w': gdn_conv_w, 'gdn_a_log': gdn_a_log,
            'gdn_dt_bias': gdn_dt_bias, 'gdn_norm_g': gdn_norm_g, 'gdn_w_out': gdn_w_out,
            'ln_mix_g': ln_mix_g, 'ln_mix_b': ln_mix_b, 'mlp_w1': mlp_w1, 'mlp_w2': mlp_w2,
            'ln_ffn_g': ln_ffn_g, 'ln_ffn_b': ln_ffn_b}


def reference(x, ret_w_in, ret_gn_g, ret_w_out, gdn_w_in, gdn_conv_w, gdn_a_log,
              gdn_dt_bias, gdn_norm_g, gdn_w_out, ln_mix_g, ln_mix_b, mlp_w1, mlp_w2,
              ln_ffn_g, ln_ffn_b):
    for i in range(DEPTH):
        j = i // N_MIXERS
        if i % N_MIXERS == 0:
            mix = retention_mixer(x, ret_w_in[j], ret_gn_g[j], ret_w_out[j])
        else:
            mix = gated_deltanet_mixer(x, gdn_w_in[j], gdn_conv_w[j], gdn_a_log[j],
                                       gdn_dt_bias[j], gdn_norm_g[j], gdn_w_out[j])
        x = layer_norm(DN_ALPHA * x + mix, ln_mix_g[i], ln_mix_b[i])
        x = layer_norm(DN_ALPHA * x + sq_relu_mlp(x, mlp_w1[i], mlp_w2[i]), ln_ffn_g[i], ln_ffn_b[i])
    return x
```

```python
import functools
import math

import jax
import jax.numpy as jnp
from jax import lax
from jax.experimental import pallas as pl
from jax.experimental.pallas import tpu as pltpu

F32 = jnp.float32
BF16 = jnp.bfloat16

D_MODEL = 2048
DEPTH = 2
CHUNK = 64

RET_HEADS = 8
RET_DK = D_MODEL // RET_HEADS
RET_DV = 2 * RET_DK
RET_QK = RET_HEADS * RET_DK
RET_VW = RET_HEADS * RET_DV
ROPE_BASE = 10000.0
GN_EPS = 1e-6

GDN_K_HEADS = 16
GDN_V_HEADS = 32
GDN_DK = 128
GDN_DV = 128
GDN_QK = GDN_K_HEADS * GDN_DK
GDN_VW = GDN_V_HEADS * GDN_DV
GDN_QKV = 2 * GDN_QK + GDN_VW
GDN_CONV = 4
RMS_EPS = 1e-6
L2_EPS = 1e-6

D_FF = 4 * D_MODEL
DN_ALPHA = (2.0 * DEPTH) ** 0.25
LN_EPS = 1e-5

V7X_LANES = 128
V7X_VMEM_LIMIT_BYTES = 56 * 1024 * 1024

MM_TM = 1024
MM_TN = 1024
LN_TM = 256
MLP_TM = 512
MLP_TF = 512
RET_BT = 256
GDN_PREP_TM = 512
GDN_PREP_TN = 1024
GDN_BA_TM = 512
GDN_KG = 4
GDN_NC = 2
GDN_BLK = 16


def _cparams(sem):
    return pltpu.CompilerParams(dimension_semantics=sem, vmem_limit_bytes=V7X_VMEM_LIMIT_BYTES)


def _layer_norm(y, g, b):
    mu = jnp.mean(y, axis=-1, keepdims=True)
    d = y - mu
    var = jnp.mean(d * d, axis=-1, keepdims=True)
    return d * lax.rsqrt(var + LN_EPS) * g + b


def _silu(x):
    return x * (1.0 / (1.0 + jnp.exp(-x)))


def _mm_kernel(x_ref, w_ref, o_ref):
    o_ref[...] = jnp.dot(x_ref[...], w_ref[...], preferred_element_type=F32).astype(o_ref.dtype)


def _matmul_cols(x, w, col_off, ncols, name):
    m, k = x.shape
    tm, tn = min(MM_TM, m), MM_TN
    off = col_off // tn
    return pl.pallas_call(
        _mm_kernel,
        grid=(ncols // tn, m // tm),
        in_specs=[pl.BlockSpec((tm, k), lambda n, i: (i, 0)),
                  pl.BlockSpec((k, tn), lambda n, i: (0, n + off))],
        out_specs=pl.BlockSpec((tm, tn), lambda n, i: (i, n)),
        out_shape=jax.ShapeDtypeStruct((m, ncols), BF16),
        compiler_params=_cparams(("parallel", "arbitrary")),
        name=name,
    )(x, w)


def _mm_rot_kernel(x_ref, w_ref, cos_ref, sin_ref, o_ref, *, n_q_tiles, k_scale, tn):
    acc = jnp.dot(x_ref[...], w_ref[...], preferred_element_type=F32)
    scale = jnp.where(pl.program_id(0) >= n_q_tiles, k_scale, 1.0).astype(F32)
    cos = cos_ref[...] * scale
    sin = sin_ref[...] * scale
    half = RET_DK // 2
    for j in range(tn // RET_DK):
        t1 = acc[:, j * RET_DK: j * RET_DK + half]
        t2 = acc[:, j * RET_DK + half: (j + 1) * RET_DK]
        o_ref[:, j * RET_DK: j * RET_DK + half] = (t1 * cos - t2 * sin).astype(o_ref.dtype)
        o_ref[:, j * RET_DK + half: (j + 1) * RET_DK] = (t1 * sin + t2 * cos).astype(o_ref.dtype)


def _ret_qk_proj(x, w, cos, sin):
    m, k = x.shape
    tm, tn = min(MM_TM, m), MM_TN
    ncols = 2 * RET_QK
    kern = functools.partial(_mm_rot_kernel, n_q_tiles=RET_QK // tn, k_scale=RET_DK ** -0.5, tn=tn)
    half = RET_DK // 2
    return pl.pallas_call(
        kern,
        grid=(ncols // tn, m // tm),
        in_specs=[pl.BlockSpec((tm, k), lambda n, i: (i, 0)),
                  pl.BlockSpec((k, tn), lambda n, i: (0, n)),
                  pl.BlockSpec((tm, half), lambda n, i: (i, 0)),
                  pl.BlockSpec((tm, half), lambda n, i: (i, 0))],
        out_specs=pl.BlockSpec((tm, tn), lambda n, i: (i, n)),
        out_shape=jax.ShapeDtypeStruct((m, ncols), BF16),
        compiler_params=_cparams(("parallel", "arbitrary")),
        name="ret_qk_proj",
    )(x, w, cos, sin)


def _ret_core_kernel(lg_ref, q_ref, k_ref, v_ref, gate_ref, gn_ref, o_ref,
                     state_ref, mask_ref, dq_ref, dk_ref):
    h = pl.program_id(0)
    t = pl.program_id(1)
    bt = q_ref.shape[0]
    lg = lg_ref[h]

    @pl.when(t == 0)
    def _init():
        state_ref[...] = jnp.zeros_like(state_ref)
        r = lax.broadcasted_iota(jnp.int32, (bt, bt), 0)
        c = lax.broadcasted_iota(jnp.int32, (bt, bt), 1)
        dist = jnp.abs(r - c).astype(F32)
        visible = (c // CHUNK) <= (r // CHUNK)
        mask_ref[...] = jnp.where(visible, jnp.exp(lg * dist), 0.0)
        rr = lax.broadcasted_iota(jnp.int32, (bt, 1), 0).astype(F32)
        dq_ref[...] = jnp.exp(lg * (rr + 1.0))
        dk_ref[...] = jnp.exp(lg * (bt - 1.0 - rr))

    q = q_ref[...]
    k = k_ref[...]
    v = v_ref[...]
    s = lax.dot_general(q, k, (((1,), (1,)), ((), ())), preferred_element_type=F32)
    p = (s * mask_ref[...]).astype(BF16)
    y = jnp.dot(p, v, preferred_element_type=F32)
    st = state_ref[...]
    y = y + dq_ref[...] * jnp.dot(q, st.astype(BF16), preferred_element_type=F32)
    kd = (k.astype(F32) * dk_ref[...]).astype(BF16)
    upd = lax.dot_general(kd, v, (((0,), (0,)), ((), ())), preferred_element_type=F32)
    state_ref[...] = st * jnp.exp(lg * bt) + upd

    mu = jnp.mean(y, axis=-1, keepdims=True)
    d = y - mu
    var = jnp.mean(d * d, axis=-1, keepdims=True)
    yn = d * lax.rsqrt(var + GN_EPS) * gn_ref[...]
    o_ref[...] = (_silu(gate_ref[...].astype(F32)) * yn).astype(o_ref.dtype)


def _ret_core(log_gamma, qk, vg, gn_g):
    s = qk.shape[0]
    bt = min(RET_BT, s)
    nqb = RET_QK // RET_DK
    nvb = RET_VW // RET_DV
    grid_spec = pltpu.PrefetchScalarGridSpec(
        num_scalar_prefetch=1,
        grid=(RET_HEADS, s // bt),
        in_specs=[pl.BlockSpec((bt, RET_DK), lambda h, t, lg: (t, h)),
                  pl.BlockSpec((bt, RET_DK), lambda h, t, lg: (t, nqb + h)),
                  pl.BlockSpec((bt, RET_DV), lambda h, t, lg: (t, h)),
                  pl.BlockSpec((bt, RET_DV), lambda h, t, lg: (t, nvb + h)),
                  pl.BlockSpec((1, RET_DV), lambda h, t, lg: (0, h))],
        out_specs=pl.BlockSpec((bt, RET_DV), lambda h, t, lg: (t, h)),
        scratch_shapes=[pltpu.VMEM((RET_DK, RET_DV), F32),
                        pltpu.VMEM((bt, bt), F32),
                        pltpu.VMEM((bt, 1), F32),
                        pltpu.VMEM((bt, 1), F32)],
    )
    return pl.pallas_call(
        _ret_core_kernel,
        grid_spec=grid_spec,
        out_shape=jax.ShapeDtypeStruct((s, RET_VW), BF16),
        compiler_params=_cparams(("parallel", "arbitrary")),
        name="ret_core",
    )(log_gamma, qk, qk, vg, vg, gn_g)


def _mm_ln_kernel(a_ref, w_ref, r_ref, g_ref, b_ref, of_ref, ob_ref):
    acc = jnp.dot(a_ref[...], w_ref[...], preferred_element_type=F32)
    out = _layer_norm(DN_ALPHA * r_ref[...] + acc, g_ref[...], b_ref[...])
    of_ref[...] = out
    ob_ref[...] = out.astype(ob_ref.dtype)


def _out_proj_ln(a, w, resid, g, b, name):
    m, k = a.shape
    n = w.shape[1]
    tm = min(LN_TM, m)
    return pl.pallas_call(
        _mm_ln_kernel,
        grid=(m // tm,),
        in_specs=[pl.BlockSpec((tm, k), lambda i: (i, 0)),
                  pl.BlockSpec((k, n), lambda i: (0, 0), pipeline_mode=pl.Buffered(1)),
                  pl.BlockSpec((tm, n), lambda i: (i, 0)),
                  pl.BlockSpec((1, n), lambda i: (0, 0)),
                  pl.BlockSpec((1, n), lambda i: (0, 0))],
        out_specs=[pl.BlockSpec((tm, n), lambda i: (i, 0)),
                   pl.BlockSpec((tm, n), lambda i: (i, 0))],
        out_shape=[jax.ShapeDtypeStruct((m, n), F32), jax.ShapeDtypeStruct((m, n), BF16)],
        compiler_params=_cparams(("arbitrary",)),
        name=name,
    )(a, w, resid, g, b)


def _mlp_kernel(xb_ref, xf_ref, w1_ref, w2_ref, g_ref, b_ref, of_ref, ob_ref, acc_ref):
    f = pl.program_id(1)
    h = jnp.dot(xb_ref[...], w1_ref[...], preferred_element_type=F32)
    h = jnp.square(jnp.maximum(h, 0.0)).astype(BF16)
    p = jnp.dot(h, w2_ref[...], preferred_element_type=F32)

    @pl.when(f == 0)
    def _first():
        acc_ref[...] = p

    @pl.when(f > 0)
    def _rest():
        acc_ref[...] += p

    @pl.when(f == pl.num_programs(1) - 1)
    def _fin():
        out = _layer_norm(DN_ALPHA * xf_ref[...] + acc_ref[...], g_ref[...], b_ref[...])
        of_ref[...] = out
        ob_ref[...] = out.astype(ob_ref.dtype)


def _mlp_ln(xb, xf, w1, w2, g, b, name):
    m, d = xb.shape
    ff = w1.shape[1]
    tm, tf = min(MLP_TM, m), MLP_TF
    return pl.pallas_call(
        _mlp_kernel,
        grid=(m // tm, ff // tf),
        in_specs=[pl.BlockSpec((tm, d), lambda i, f: (i, 0)),
                  pl.BlockSpec((tm, d), lambda i, f: (i, 0)),
                  pl.BlockSpec((d, tf), lambda i, f: (0, f)),
                  pl.BlockSpec((tf, d), lambda i, f: (f, 0)),
                  pl.BlockSpec((1, d), lambda i, f: (0, 0)),
                  pl.BlockSpec((1, d), lambda i, f: (0, 0))],
        out_specs=[pl.BlockSpec((tm, d), lambda i, f: (i, 0)),
                   pl.BlockSpec((tm, d), lambda i, f: (i, 0))],
        out_shape=[jax.ShapeDtypeStruct((m, d), F32), jax.ShapeDtypeStruct((m, d), BF16)],
        scratch_shapes=[pltpu.VMEM((tm, d), F32)],
        compiler_params=_cparams(("parallel", "arbitrary")),
        name=name,
    )(xb, xf, w1, w2, g, b)


def _gdn_ba_kernel(x_ref, w_ref, alog_ref, dtb_ref, o_ref, ot_ref, *, gsz):
    acc = jnp.dot(x_ref[...], w_ref[...], preferred_element_type=F32)
    tm, nc = acc.shape
    acc = jnp.concatenate([acc, jnp.zeros((tm, V7X_LANES - nc), F32)], axis=1)
    lane = lax.broadcasted_iota(jnp.int32, acc.shape, 1)
    is_beta = (lane % (2 * gsz)) < gsz
    beta = 1.0 / (1.0 + jnp.exp(-acc))
    a = acc + dtb_ref[...]
    softplus = jnp.maximum(a, 0.0) + jnp.log(1.0 + jnp.exp(-jnp.abs(a)))
    g = -jnp.exp(alog_ref[...]) * softplus
    row = lax.broadcasted_iota(jnp.int32, acc.shape, 0) % CHUNK
    sh = 1
    while sh < CHUNK:
        g = g + jnp.where(row >= sh, pltpu.roll(g, sh, 0), 0.0)
        sh *= 2
    res = jnp.where(is_beta, beta, g)
    res_t = res.T
    for gi in range(nc // (2 * gsz)):
        o_ref[gi] = res[:, gi * 2 * gsz:(gi + 1) * 2 * gsz]
        ot_ref[gi] = res_t[gi * 2 * gsz:(gi + 1) * 2 * gsz, :]


def _gdn_beta_decay(xb, w_ba, a_log, dt_bias, gsz):
    m, k = xb.shape
    tm = min(GDN_BA_TM, m)
    nc = 2 * GDN_V_HEADS
    ng = GDN_V_HEADS // gsz
    kern = functools.partial(_gdn_ba_kernel, gsz=gsz)
    return pl.pallas_call(
        kern,
        grid=(m // tm,),
        in_specs=[pl.BlockSpec((tm, k), lambda i: (i, 0)),
                  pl.BlockSpec((k, nc), lambda i: (0, 0)),
                  pl.BlockSpec((1, V7X_LANES), lambda i: (0, 0)),
                  pl.BlockSpec((1, V7X_LANES), lambda i: (0, 0))],
        out_specs=[pl.BlockSpec((ng, tm, 2 * gsz), lambda i: (0, i, 0)),
                   pl.BlockSpec((ng, 2 * gsz, tm), lambda i: (0, 0, i))],
        out_shape=[jax.ShapeDtypeStruct((ng, m, 2 * gsz), F32),
                   jax.ShapeDtypeStruct((ng, 2 * gsz, m), F32)],
        compiler_params=_cparams(("parallel",)),
        name="gdn_beta_decay",
    )(xb, w_ba, a_log, dt_bias)


def _gdn_prep_kernel(cur_ref, prev_ref, cw_ref, o_ref, ext_ref, *, normalize, scale):
    i = pl.program_id(1)
    tm, tn = cur_ref.shape
    halo = prev_ref.shape[0]
    pad = 8
    prev = prev_ref[...].astype(F32)[halo - pad:, :]
    ext_ref[0:pad, :] = jnp.where(i > 0, prev, 0.0)
    ext_ref[pad:, :] = cur_ref[...].astype(F32)
    cw = cw_ref[...]
    acc = ext_ref[pad:, :] * cw[GDN_CONV - 1:GDN_CONV, :]
    for d in range(1, GDN_CONV):
        acc = acc + ext_ref[pad - d: pad - d + tm, :] * cw[GDN_CONV - 1 - d: GDN_CONV - d, :]
    y = _silu(acc)
    if normalize:
        for j in range(tn // GDN_DK):
            blk = y[:, j * GDN_DK:(j + 1) * GDN_DK]
            ss = jnp.sum(blk * blk, axis=-1, keepdims=True)
            o_ref[:, j * GDN_DK:(j + 1) * GDN_DK] = (
                blk * (lax.rsqrt(ss + L2_EPS) * scale)).astype(o_ref.dtype)
    else:
        o_ref[...] = y.astype(o_ref.dtype)


def _gdn_prep(qkv, conv_w, col_off, ncols, normalize, scale, name):
    s = qkv.shape[0]
    tm, tn = min(GDN_PREP_TM, s), GDN_PREP_TN
    halo = 16
    off = col_off // tn
    kern = functools.partial(_gdn_prep_kernel, normalize=normalize, scale=scale)
    return pl.pallas_call(
        kern,
        grid=(ncols // tn, s // tm),
        in_specs=[pl.BlockSpec((tm, tn), lambda n, i: (i, n + off)),
                  pl.BlockSpec((halo, tn),
                               lambda n, i: (jnp.maximum(i * (tm // halo) - 1, 0), n + off)),
                  pl.BlockSpec((GDN_CONV, tn), lambda n, i: (0, n + off))],
        out_specs=pl.BlockSpec((tm, tn), lambda n, i: (i, n)),
        out_shape=jax.ShapeDtypeStruct((s, ncols), BF16),
        scratch_shapes=[pltpu.VMEM((tm + 8, tn), F32)],
        compiler_params=_cparams(("parallel", "arbitrary")),
        name=name,
    )(qkv, qkv, conv_w)


def _mxu(a, b):
    return jnp.dot(a.astype(BF16), b.astype(BF16), preferred_element_type=F32)


def _unit_lower_inverse(a_strict, eye, blk_mask):
    ad = jnp.where(blk_mask, a_strict, 0.0)
    ao = a_strict - ad
    x2 = _mxu(ad, ad)
    x4 = _mxu(x2, x2)
    x8 = _mxu(x4, x4)
    dinv = _mxu(eye - ad, eye + x2)
    dinv = _mxu(dinv, eye + x4)
    dinv = _mxu(dinv, eye + x8)
    n1 = _mxu(dinv, ao)
    n2 = _mxu(n1, n1)
    t = _mxu(eye - n1, eye + n2)
    return _mxu(t, dinv)


def _gdn_core_kernel(q_ref, k_ref, v_ref, z_ref, bg_ref, bgt_ref, ng_ref, o_ref, state_ref,
                     *, kg, nc):
    t = pl.program_id(1)
    c = CHUNK
    gsz = 2 * kg

    @pl.when(t == 0)
    def _init():
        state_ref[...] = jnp.zeros_like(state_ref)

    r = lax.broadcasted_iota(jnp.int32, (c, c), 0)
    cc = lax.broadcasted_iota(jnp.int32, (c, c), 1)
    causal = r >= cc
    strict = r > cc
    eye = jnp.where(r == cc, 1.0, 0.0).astype(F32)
    blk_mask = (r // GDN_BLK) == (cc // GDN_BLK)
    ng = ng_ref[...]

    bg_all = bg_ref[...]
    bgt_all = bgt_ref[...]

    for ci in range(nc):
        rows = slice(ci * c, (ci + 1) * c)
        for kh in range(kg):
            kcols = slice(kh * GDN_DK, (kh + 1) * GDN_DK)
            q = q_ref[rows, kcols]
            k = k_ref[rows, kcols]
            kf = k.astype(F32)
            kkt = lax.dot_general(k, k, (((1,), (1,)), ((), ())), preferred_element_type=F32)
            qkt = lax.dot_general(q, k, (((1,), (1,)), ((), ())), preferred_element_type=F32)
            for vi in range(2):
                hl = 2 * kh + vi
                vcols = slice(hl * GDN_DV, (hl + 1) * GDN_DV)
                beta_col = bg_all[rows, hl:hl + 1]
                gc_col = bg_all[rows, gsz + hl:gsz + hl + 1]
                gc_row = bgt_all[gsz + hl:gsz + hl + 1, rows]
                gc_last = gc_col[c - 1:c, :]

                dec = jnp.exp(jnp.where(causal, gc_col - gc_row, -jnp.inf))
                a_strict = jnp.where(strict, beta_col * kkt * dec, 0.0)
                tmat = _unit_lower_inverse(a_strict, eye, blk_mask)

                v = v_ref[rows, vcols].astype(F32)
                egc = jnp.exp(gc_col)
                rhs = jnp.concatenate([v * beta_col, kf * (beta_col * egc)], axis=1)
                uw = _mxu(tmat, rhs)
                u = uw[:, :GDN_DV]
                w = uw[:, GDN_DV:]
                attn = qkt * dec

                st = state_ref[hl]
                stb = st.astype(BF16)
                v_new = u - _mxu(w, stb)
                y = egc * jnp.dot(q, stb, preferred_element_type=F32) + _mxu(attn, v_new)
                kdec = kf * jnp.exp(gc_last - gc_col)
                upd = lax.dot_general(kdec.astype(BF16), v_new.astype(BF16),
                                      (((0,), (0,)), ((), ())), preferred_element_type=F32)
                state_ref[hl] = st * jnp.exp(gc_last) + upd

                ms = jnp.mean(y * y, axis=-1, keepdims=True)
                yn = y * lax.rsqrt(ms + RMS_EPS) * ng
                o_ref[rows, vcols] = (yn * _silu(z_ref[rows, vcols].astype(F32))).astype(o_ref.dtype)


def _gdn_core(q, k, v, qkvz, bg, bgt, norm_g):
    s = q.shape[0]
    kg, nc = GDN_KG, GDN_NC
    bt = nc * CHUNK
    kern = functools.partial(_gdn_core_kernel, kg=kg, nc=nc)
    vw = 2 * kg * GDN_DV
    z_off = GDN_QKV // vw
    return pl.pallas_call(
        kern,
        grid=(GDN_K_HEADS // kg, s // bt),
        in_specs=[pl.BlockSpec((bt, kg * GDN_DK), lambda g, t: (t, g)),
                  pl.BlockSpec((bt, kg * GDN_DK), lambda g, t: (t, g)),
                  pl.BlockSpec((bt, vw), lambda g, t: (t, g)),
                  pl.BlockSpec((bt, vw), lambda g, t: (t, g + z_off)),
                  pl.BlockSpec((None, bt, 4 * kg), lambda g, t: (g, t, 0)),
                  pl.BlockSpec((None, 4 * kg, bt), lambda g, t: (g, 0, t)),
                  pl.BlockSpec((1, GDN_DV), lambda g, t: (0, 0))],
        out_specs=pl.BlockSpec((bt, vw), lambda g, t: (t, g)),
        out_shape=jax.ShapeDtypeStruct((s, GDN_VW), BF16),
        scratch_shapes=[pltpu.VMEM((2 * kg, GDN_DK, GDN_DV), F32)],
        compiler_params=_cparams(("parallel", "arbitrary")),
        name="gdn_core",
    )(q, k, v, qkvz, bg, bgt, norm_g)


def _retention_mixer(xb, xf, w_in, gn_g, w_out, ln_g, ln_b):
    s = xb.shape[0]
    half = RET_DK // 2
    inv = ROPE_BASE ** (-jnp.arange(half, dtype=F32) / half)
    ang = jnp.arange(s).astype(F32)[:, None] * inv[None, :]
    cos, sin = jnp.cos(ang), jnp.sin(ang)
    log_gamma = jnp.log1p(-jnp.exp2(-5.0 - jnp.arange(RET_HEADS, dtype=F32)))

    wb = w_in.astype(BF16)
    qk = _ret_qk_proj(xb, wb, cos, sin)
    vg = _matmul_cols(xb, wb, 2 * RET_QK, 2 * RET_VW, "ret_vg_proj")
    a = _ret_core(log_gamma, qk, vg, gn_g.reshape(1, RET_VW))
    return _out_proj_ln(a, w_out.astype(BF16), xf, ln_g.reshape(1, -1), ln_b.reshape(1, -1),
                        "ret_out_ln")


def _gdn_mixer(xb, xf, w_in, conv_w, a_log, dt_bias, norm_g, w_out, ln_g, ln_b):
    nh, gsz = GDN_V_HEADS, 2 * GDN_KG
    ng = nh // gsz
    wb = w_in[:, :GDN_QKV + GDN_VW].astype(BF16)
    w_b = w_in[:, GDN_QKV + GDN_VW:GDN_QKV + GDN_VW + nh].reshape(-1, ng, gsz)
    w_a = w_in[:, GDN_QKV + GDN_VW + nh:].reshape(-1, ng, gsz)
    w_ba = jnp.concatenate([w_b, w_a], axis=2).reshape(-1, 2 * nh).astype(BF16)
    zeros = jnp.zeros((ng, gsz), F32)

    def _lanes(vec):
        tab = jnp.concatenate([zeros, vec.astype(F32).reshape(ng, gsz)], axis=1).reshape(1, 2 * nh)
        return jnp.pad(tab, ((0, 0), (0, V7X_LANES - 2 * nh)))

    qkvz = _matmul_cols(xb, wb, 0, GDN_QKV + GDN_VW, "gdn_proj")
    bg, bgt = _gdn_beta_decay(xb, w_ba, _lanes(a_log), _lanes(dt_bias), gsz)
    q = _gdn_prep(qkvz, conv_w, 0, GDN_QK, True, GDN_DK ** -0.5, "gdn_prep_q")
    k = _gdn_prep(qkvz, conv_w, GDN_QK, GDN_QK, True, 1.0, "gdn_prep_k")
    v = _gdn_prep(qkvz, conv_w, 2 * GDN_QK, GDN_VW, False, 1.0, "gdn_prep_v")
    a = _gdn_core(q, k, v, qkvz, bg, bgt, norm_g.reshape(1, GDN_DV))
    return _out_proj_ln(a, w_out.astype(BF16), xf, ln_g.reshape(1, -1), ln_b.reshape(1, -1),
                        "gdn_out_ln")


def kernel(x, ret_w_in, ret_gn_g, ret_w_out, gdn_w_in, gdn_conv_w, gdn_a_log, gdn_dt_bias,
           gdn_norm_g, gdn_w_out, ln_mix_g, ln_mix_b, mlp_w1, mlp_w2, ln_ffn_g, ln_ffn_b):
    b, s, d = x.shape
    outs = []
    for bi in range(b):
        xf = x[bi]
        xb = xf.astype(BF16)
        for i in range(DEPTH):
            j = i // 2
            if i % 2 == 0:
                xf, xb = _retention_mixer(xb, xf, ret_w_in[j], ret_gn_g[j], ret_w_out[j],
                                          ln_mix_g[i], ln_mix_b[i])
            else:
                xf, xb = _gdn_mixer(xb, xf, gdn_w_in[j], gdn_conv_w[j], gdn_a_log[j],
                                    gdn_dt_bias[j], gdn_norm_g[j], gdn_w_out[j],
                                    ln_mix_g[i], ln_mix_b[i])
            xf, xb = _mlp_ln(xb, xf, mlp_w1[i].astype(BF16), mlp_w2[i].astype(BF16),
                             ln_ffn_g[i].reshape(1, -1), ln_ffn_b[i].reshape(1, -1), f"mlp_ln_{i}")
        outs.append(xf)
    return jnp.stack(outs, axis=0)
```

```python
import functools
import math

import jax
import jax.numpy as jnp
from jax import lax
from jax.experimental import pallas as pl
from jax.experimental.pallas import tpu as pltpu

F32 = jnp.float32
BF16 = jnp.bfloat16

D_MODEL = 2048
DEPTH = 2
CHUNK = 64

RET_HEADS = 8
RET_DK = D_MODEL // RET_HEADS
RET_DV = 2 * RET_DK
RET_QK = RET_HEADS * RET_DK
RET_VW = RET_HEADS * RET_DV
ROPE_BASE = 10000.0
GN_EPS = 1e-6

GDN_K_HEADS = 16
GDN_V_HEADS = 32
GDN_DK = 128
GDN_DV = 128
GDN_QK = GDN_K_HEADS * GDN_DK
GDN_VW = GDN_V_HEADS * GDN_DV
GDN_QKV = 2 * GDN_QK + GDN_VW
GDN_CONV = 4
RMS_EPS = 1e-6
L2_EPS = 1e-6

D_FF = 4 * D_MODEL
DN_ALPHA = (2.0 * DEPTH) ** 0.25
LN_EPS = 1e-5

V7X_LANES = 128
V7X_VMEM_LIMIT_BYTES = 56 * 1024 * 1024

MM_TM = 1024
MM_TN = 1024
LN_TM = 256
MLP_TM = 512
MLP_TF = 512
RET_BT = 256
GDN_PREP_TM = 512
GDN_PREP_TN = 1024
GDN_BA_TM = 512
GDN_KG = 4
GDN_NC = 2
GDN_BLK = 16


def _cparams(sem):
    return pltpu.CompilerParams(dimension_semantics=sem, vmem_limit_bytes=V7X_VMEM_LIMIT_BYTES)


def _layer_norm(y, g, b):
    mu = jnp.mean(y, axis=-1, keepdims=True)
    d = y - mu
    var = jnp.mean(d * d, axis=-1, keepdims=True)
    return d * lax.rsqrt(var + LN_EPS) * g + b


def _silu(x):
    return x * (1.0 / (1.0 + jnp.exp(-x)))


def _mm_kernel(x_ref, w_ref, o_ref):
    o_ref[...] = jnp.dot(x_ref[...], w_ref[...], preferred_element_type=F32).astype(o_ref.dtype)


def _matmul_cols(x, w, col_off, ncols, name):
    m, k = x.shape
    tm, tn = min(MM_TM, m), MM_TN
    off = col_off // tn
    return pl.pallas_call(
        _mm_kernel,
        grid=(ncols // tn, m // tm),
        in_specs=[pl.BlockSpec((tm, k), lambda n, i: (i, 0)),
                  pl.BlockSpec((k, tn), lambda n, i: (0, n + off))],
        out_specs=pl.BlockSpec((tm, tn), lambda n, i: (i, n)),
        out_shape=jax.ShapeDtypeStruct((m, ncols), BF16),
        compiler_params=_cparams(("parallel", "arbitrary")),
        name=name,
    )(x, w)


def _mm_rot_kernel(x_ref, w_ref, cos_ref, sin_ref, o_ref, *, n_q_tiles, k_scale, tn):
    acc = jnp.dot(x_ref[...], w_ref[...], preferred_element_type=F32)
    scale = jnp.where(pl.program_id(0) >= n_q_tiles, k_scale, 1.0).astype(F32)
    cos = cos_ref[...] * scale
    sin = sin_ref[...] * scale
    half = RET_DK // 2
    for j in range(tn // RET_DK):
        t1 = acc[:, j * RET_DK: j * RET_DK + half]
        t2 = acc[:, j * RET_DK + half: (j + 1) * RET_DK]
        o_ref[:, j * RET_DK: j * RET_DK + half] = (t1 * cos - t2 * sin).astype(o_ref.dtype)
        o_ref[:, j * RET_DK + half: (j + 1) * RET_DK] = (t1 * sin + t2 * cos).astype(o_ref.dtype)


def _ret_qk_proj(x, w, cos, sin):
    m, k = x.shape
    tm, tn = min(MM_TM, m), MM_TN
    ncols = 2 * RET_QK
    kern = functools.partial(_mm_rot_kernel, n_q_tiles=RET_QK // tn, k_scale=RET_DK ** -0.5, tn=tn)
    half = RET_DK // 2
    return pl.pallas_call(
        kern,
        grid=(ncols // tn, m // tm),
        in_specs=[pl.BlockSpec((tm, k), lambda n, i: (i, 0)),
                  pl.BlockSpec((k, tn), lambda n, i: (0, n)),
                  pl.BlockSpec((tm, half), lambda n, i: (i, 0)),
                  pl.BlockSpec((tm, half), lambda n, i: (i, 0))],
        out_specs=pl.BlockSpec((tm, tn), lambda n, i: (i, n)),
        out_shape=jax.ShapeDtypeStruct((m, ncols), BF16),
        compiler_params=_cparams(("parallel", "arbitrary")),
        name="ret_qk_proj",
    )(x, w, cos, sin)


def _ret_core_kernel(lg_ref, q_ref, k_ref, v_ref, gate_ref, gn_ref, o_ref,
                     state_ref, mask_ref, dq_ref, dk_ref):
    h = pl.program_id(0)
    t = pl.program_id(1)
    bt = q_ref.shape[0]
    lg = lg_ref[h]

    @pl.when(t == 0)
    def _init():
        state_ref[...] = jnp.zeros_like(state_ref)
        r = lax.broadcasted_iota(jnp.int32, (bt, bt), 0)
        c = lax.broadcasted_iota(jnp.int32, (bt, bt), 1)
        dist = jnp.abs(r - c).astype(F32)
        visible = (c // CHUNK) <= (r // CHUNK)
        mask_ref[...] = jnp.where(visible, jnp.exp(lg * dist), 0.0)
        rr = lax.broadcasted_iota(jnp.int32, (bt, 1), 0).astype(F32)
        dq_ref[...] = jnp.exp(lg * (rr + 1.0))
        dk_ref[...] = jnp.exp(lg * (bt - 1.0 - rr))

    q = q_ref[...]
    k = k_ref[...]
    v = v_ref[...]
    s = lax.dot_general(q, k, (((1,), (1,)), ((), ())), preferred_element_type=F32)
    p = (s * mask_ref[...]).astype(BF16)
    y = jnp.dot(p, v, preferred_element_type=F32)
    st = state_ref[...]
    y = y + dq_ref[...] * jnp.dot(q, st.astype(BF16), preferred_element_type=F32)
    kd = (k.astype(F32) * dk_ref[...]).astype(BF16)
    upd = lax.dot_general(kd, v, (((0,), (0,)), ((), ())), preferred_element_type=F32)
    state_ref[...] = st * jnp.exp(lg * bt) + upd

    mu = jnp.mean(y, axis=-1, keepdims=True)
    d = y - mu
    var = jnp.mean(d * d, axis=-1, keepdims=True)
    yn = d * lax.rsqrt(var + GN_EPS) * gn_ref[...]
    o_ref[...] = (_silu(gate_ref[...].astype(F32)) * yn).astype(o_ref.dtype)


def _ret_core(log_gamma, qk, vg, gn_g):
    s = qk.shape[0]
    bt = min(RET_BT, s)
    nqb = RET_QK // RET_DK
    nvb = RET_VW // RET_DV
    grid_spec = pltpu.PrefetchScalarGridSpec(
        num_scalar_prefetch=1,
        grid=(RET_HEADS, s // bt),
        in_specs=[pl.BlockSpec((bt, RET_DK), lambda h, t, lg: (t, h)),
                  pl.BlockSpec((bt, RET_DK), lambda h, t, lg: (t, nqb + h)),
                  pl.BlockSpec((bt, RET_DV), lambda h, t, lg: (t, h)),
                  pl.BlockSpec((bt, RET_DV), lambda h, t, lg: (t, nvb + h)),
                  pl.BlockSpec((1, RET_DV), lambda h, t, lg: (0, h))],
        out_specs=pl.BlockSpec((bt, RET_DV), lambda h, t, lg: (t, h)),
        scratch_shapes=[pltpu.VMEM((RET_DK, RET_DV), F32),
                        pltpu.VMEM((bt, bt), F32),
                        pltpu.VMEM((bt, 1), F32),
                        pltpu.VMEM((bt, 1), F32)],
    )
    return pl.pallas_call(
        _ret_core_kernel,
        grid_spec=grid_spec,
        out_shape=jax.ShapeDtypeStruct((s, RET_VW), BF16),
        compiler_params=_cparams(("parallel", "arbitrary")),
        name="ret_core",
    )(log_gamma, qk, qk, vg, vg, gn_g)


def _mm_ln_kernel(a_ref, w_ref, r_ref, g_ref, b_ref, of_ref, ob_ref):
    acc = jnp.dot(a_ref[...], w_ref[...], preferred_element_type=F32)
    out = _layer_norm(DN_ALPHA * r_ref[...] + acc, g_ref[...], b_ref[...])
    of_ref[...] = out
    ob_ref[...] = out.astype(ob_ref.dtype)


def _out_proj_ln(a, w, resid, g, b, name):
    m, k = a.shape
    n = w.shape[1]
    tm = min(LN_TM, m)
    return pl.pallas_call(
        _mm_ln_kernel,
        grid=(m // tm,),
        in_specs=[pl.BlockSpec((tm, k), lambda i: (i, 0)),
                  pl.BlockSpec((k, n), lambda i: (0, 0), pipeline_mode=pl.Buffered(1)),
                  pl.BlockSpec((tm, n), lambda i: (i, 0)),
                  pl.BlockSpec((1, n), lambda i: (0, 0)),
                  pl.BlockSpec((1, n), lambda i: (0, 0))],
        out_specs=[pl.BlockSpec((tm, n), lambda i: (i, 0)),
                   pl.BlockSpec((tm, n), lambda i: (i, 0))],
        out_shape=[jax.ShapeDtypeStruct((m, n), F32), jax.ShapeDtypeStruct((m, n), BF16)],
        compiler_params=_cparams(("arbitrary",)),
        name=name,
    )(a, w, resid, g, b)


def _mlp_kernel(xb_ref, xf_ref, w1_ref, w2_ref, g_ref, b_ref, of_ref, ob_ref, acc_ref):
    f = pl.program_id(1)

    @pl.when(f == 0)
    def _first():
        acc_ref[...] = jnp.zeros_like(acc_ref)

    h = jnp.dot(xb_ref[...], w1_ref[...], preferred_element_type=F32)
    h = jnp.square(jnp.maximum(h, 0.0)).astype(BF16)
    acc_ref[...] += jnp.dot(h, w2_ref[...], preferred_element_type=F32)

    @pl.when(f == pl.num_programs(1) - 1)
    def _fin():
        out = _layer_norm(DN_ALPHA * xf_ref[...] + acc_ref[...], g_ref[...], b_ref[...])
        of_ref[...] = out
        ob_ref[...] = out.astype(ob_ref.dtype)


def _mlp_ln(xb, xf, w1, w2, g, b, name):
    m, d = xb.shape
    ff = w1.shape[1]
    tm, tf = min(MLP_TM, m), MLP_TF
    return pl.pallas_call(
        _mlp_kernel,
        grid=(m // tm, ff // tf),
        in_specs=[pl.BlockSpec((tm, d), lambda i, f: (i, 0)),
                  pl.BlockSpec((tm, d), lambda i, f: (i, 0)),
                  pl.BlockSpec((d, tf), lambda i, f: (0, f)),
                  pl.BlockSpec((tf, d), lambda i, f: (f, 0)),
                  pl.BlockSpec((1, d), lambda i, f: (0, 0)),
                  pl.BlockSpec((1, d), lambda i, f: (0, 0))],
        out_specs=[pl.BlockSpec((tm, d), lambda i, f: (i, 0)),
                   pl.BlockSpec((tm, d), lambda i, f: (i, 0))],
        out_shape=[jax.ShapeDtypeStruct((m, d), F32), jax.ShapeDtypeStruct((m, d), BF16)],
        scratch_shapes=[pltpu.VMEM((tm, d), F32)],
        compiler_params=_cparams(("parallel", "arbitrary")),
        name=name,
    )(xb, xf, w1, w2, g, b)


def _gdn_ba_kernel(x_ref, w_ref, alog_ref, dtb_ref, o_ref, ot_ref, *, gsz):
    acc = jnp.dot(x_ref[...], w_ref[...], preferred_element_type=F32)
    tm, nc = acc.shape
    acc = jnp.concatenate([acc, jnp.zeros((tm, V7X_LANES - nc), F32)], axis=1)
    lane = lax.broadcasted_iota(jnp.int32, acc.shape, 1)
    is_beta = (lane % (2 * gsz)) < gsz
    beta = 1.0 / (1.0 + jnp.exp(-acc))
    a = acc + dtb_ref[...]
    softplus = jnp.maximum(a, 0.0) + jnp.log(1.0 + jnp.exp(-jnp.abs(a)))
    g = -jnp.exp(alog_ref[...]) * softplus
    row = lax.broadcasted_iota(jnp.int32, acc.shape, 0) % CHUNK
    sh = 1
    while sh < CHUNK:
        g = g + jnp.where(row >= sh, pltpu.roll(g, sh, 0), 0.0)
        sh *= 2
    res = jnp.where(is_beta, beta, g)
    res_t = res.T
    for gi in range(nc // (2 * gsz)):
        o_ref[gi] = res[:, gi * 2 * gsz:(gi + 1) * 2 * gsz]
        ot_ref[gi] = res_t[gi * 2 * gsz:(gi + 1) * 2 * gsz, :]


def _gdn_beta_decay(xb, w_ba, a_log, dt_bias, gsz):
    m, k = xb.shape
    tm = min(GDN_BA_TM, m)
    nc = 2 * GDN_V_HEADS
    ng = GDN_V_HEADS // gsz
    kern = functools.partial(_gdn_ba_kernel, gsz=gsz)
    return pl.pallas_call(
        kern,
        grid=(m // tm,),
        in_specs=[pl.BlockSpec((tm, k), lambda i: (i, 0)),
                  pl.BlockSpec((k, nc), lambda i: (0, 0)),
                  pl.BlockSpec((1, V7X_LANES), lambda i: (0, 0)),
                  pl.BlockSpec((1, V7X_LANES), lambda i: (0, 0))],
        out_specs=[pl.BlockSpec((ng, tm, 2 * gsz), lambda i: (0, i, 0)),
                   pl.BlockSpec((ng, 2 * gsz, tm), lambda i: (0, 0, i))],
        out_shape=[jax.ShapeDtypeStruct((ng, m, 2 * gsz), F32),
                   jax.ShapeDtypeStruct((ng, 2 * gsz, m), F32)],
        compiler_params=_cparams(("parallel",)),
        name="gdn_beta_decay",
    )(xb, w_ba, a_log, dt_bias)


def _gdn_prep_kernel(cur_ref, prev_ref, cw_ref, o_ref, ext_ref, *, normalize, scale):
    i = pl.program_id(1)
    tm, tn = cur_ref.shape
    halo = prev_ref.shape[0]
    pad = 8
    prev = prev_ref[...].astype(F32)[halo - pad:, :]
    ext_ref[0:pad, :] = jnp.where(i > 0, prev, 0.0)
    ext_ref[pad:, :] = cur_ref[...].astype(F32)
    cw = cw_ref[...]
    acc = ext_ref[pad:, :] * cw[GDN_CONV - 1:GDN_CONV, :]
    for d in range(1, GDN_CONV):
        acc = acc + ext_ref[pad - d: pad - d + tm, :] * cw[GDN_CONV - 1 - d: GDN_CONV - d, :]
    y = _silu(acc)
    if normalize:
        for j in range(tn // GDN_DK):
            blk = y[:, j * GDN_DK:(j + 1) * GDN_DK]
            ss = jnp.sum(blk * blk, axis=-1, keepdims=True)
            o_ref[:, j * GDN_DK:(j + 1) * GDN_DK] = (
                blk * (lax.rsqrt(ss + L2_EPS) * scale)).astype(o_ref.dtype)
    else:
        o_ref[...] = y.astype(o_ref.dtype)


def _gdn_prep(qkv, conv_w, col_off, ncols, normalize, scale, name):
    s = qkv.shape[0]
    tm, tn = min(GDN_PREP_TM, s), GDN_PREP_TN
    halo = 16
    off = col_off // tn
    kern = functools.partial(_gdn_prep_kernel, normalize=normalize, scale=scale)
    return pl.pallas_call(
        kern,
        grid=(ncols // tn, s // tm),
        in_specs=[pl.BlockSpec((tm, tn), lambda n, i: (i, n + off)),
                  pl.BlockSpec((halo, tn),
                               lambda n, i: (jnp.maximum(i * (tm // halo) - 1, 0), n + off)),
                  pl.BlockSpec((GDN_CONV, tn), lambda n, i: (0, n + off))],
        out_specs=pl.BlockSpec((tm, tn), lambda n, i: (i, n)),
        out_shape=jax.ShapeDtypeStruct((s, ncols), BF16),
        scratch_shapes=[pltpu.VMEM((tm + 8, tn), F32)],
        compiler_params=_cparams(("parallel", "arbitrary")),
        name=name,
    )(qkv, qkv, conv_w)


def _pair_blockdiag(x2, left):
    top = jnp.where(left, x2, 0.0).astype(BF16)
    bot = jnp.where(left, 0.0, x2).astype(BF16)
    return jnp.concatenate([top, bot], axis=0)


def _gdn_core_kernel(q_ref, k_ref, v_ref, z_ref, bg_ref, bgt_ref, ng_ref, o_ref, state_ref,
                     *, kg, nc):
    t = pl.program_id(1)
    c = CHUNK
    gsz = 2 * kg
    dv = GDN_DV

    @pl.when(t == 0)
    def _init():
        state_ref[...] = jnp.zeros_like(state_ref)

    r = lax.broadcasted_iota(jnp.int32, (c, 2 * c), 0)
    lane = lax.broadcasted_iota(jnp.int32, (c, 2 * c), 1)
    left = lane < c
    col = lane & (c - 1)
    causal = r >= col
    strict = r > col
    eye2 = jnp.where(r == col, 1.0, 0.0).astype(F32)
    blk_mask = (r // GDN_BLK) == (col // GDN_BLK)
    ng = ng_ref[...]
    zeros_v = jnp.zeros((c, dv), BF16)

    bg_all = bg_ref[...]
    bgt_all = bgt_ref[...]

    pairs = [(ci, kh) for ci in range(nc) for kh in range(kg)]

    def pmm(x2, y2):
        return jnp.dot(x2.astype(BF16), _pair_blockdiag(y2, left), preferred_element_type=F32)

    qs, ks, a2s, attn2s, betas, egcs, ekls, cdecs = {}, {}, {}, {}, {}, {}, {}, {}
    for p in pairs:
        ci, kh = p
        rows = slice(ci * c, (ci + 1) * c)
        kcols = slice(kh * GDN_DK, (kh + 1) * GDN_DK)
        q = q_ref[rows, kcols]
        k = k_ref[rows, kcols]
        qs[p], ks[p] = q, k
        qk = jnp.concatenate([q, k], axis=0)
        kk2 = jnp.concatenate([k, k], axis=0)
        prod = lax.dot_general(qk, kk2, (((1,), (1,)), ((), ())), preferred_element_type=F32)
        qkt2, kkt2 = prod[:c], prod[c:]
        bcol, gcol, grow, glast = [], [], [], []
        for vi in range(2):
            hl = 2 * kh + vi
            bcol.append(jnp.broadcast_to(bg_all[rows, hl:hl + 1], (c, dv)))
            gcol.append(jnp.broadcast_to(bg_all[rows, gsz + hl:gsz + hl + 1], (c, dv)))
            grow.append(bgt_all[gsz + hl:gsz + hl + 1, rows])
            glast.append(gcol[vi][c - 1:c, :])
        beta2 = jnp.where(left, bcol[0], bcol[1])
        gc2 = jnp.where(left, gcol[0], gcol[1])
        grow2 = jnp.concatenate(grow, axis=1)
        dec2 = jnp.exp(jnp.where(causal, gc2 - grow2, -jnp.inf))
        a2s[p] = jnp.where(strict, beta2 * kkt2 * dec2, 0.0)
        attn2s[p] = qkt2 * dec2
        betas[p] = bcol
        egcs[p] = [jnp.exp(gcol[0]), jnp.exp(gcol[1])]
        ekls[p] = [jnp.exp(glast[0] - gcol[0]), jnp.exp(glast[1] - gcol[1])]
        cdecs[p] = jnp.concatenate([jnp.exp(glast[0]), jnp.exp(glast[1])], axis=1)

    ad = {p: jnp.where(blk_mask, a2s[p], 0.0) for p in pairs}
    ao = {p: a2s[p] - ad[p] for p in pairs}
    x2 = {p: pmm(ad[p], ad[p]) for p in pairs}
    x4 = {p: pmm(x2[p], x2[p]) for p in pairs}
    x8 = {p: pmm(x4[p], x4[p]) for p in pairs}
    dinv = {p: pmm(eye2 - ad[p], eye2 + x2[p]) for p in pairs}
    dinv = {p: pmm(dinv[p], eye2 + x4[p]) for p in pairs}
    dinv = {p: pmm(dinv[p], eye2 + x8[p]) for p in pairs}
    n1 = {p: pmm(dinv[p], ao[p]) for p in pairs}
    n2 = {p: pmm(n1[p], n1[p]) for p in pairs}
    tm = {p: pmm(eye2 - n1[p], eye2 + n2[p]) for p in pairs}
    tm = {p: pmm(tm[p], dinv[p]) for p in pairs}

    us, ws = {}, {}
    for p in pairs:
        ci, kh = p
        rows = slice(ci * c, (ci + 1) * c)
        kf = ks[p].astype(F32)
        tb = tm[p].astype(BF16)
        u2, w2 = [], []
        for vi in range(2):
            hl = 2 * kh + vi
            v = v_ref[rows, hl * dv:(hl + 1) * dv].astype(F32)
            beta = betas[p][vi]
            rhs = jnp.concatenate([v * beta, kf * (beta * egcs[p][vi])], axis=1).astype(BF16)
            uw = jnp.dot(tb[:, vi * c:(vi + 1) * c], rhs, preferred_element_type=F32)
            u2.append(uw[:, :dv])
            w2.append(uw[:, dv:])
        us[p] = jnp.concatenate(u2, axis=1)
        ws[p] = jnp.concatenate(w2, axis=1).astype(BF16)

    for ci in range(nc):
        rows = slice(ci * c, (ci + 1) * c)
        st = {kh: state_ref[kh] for kh in range(kg)}
        stb = {kh: st[kh].astype(BF16) for kh in range(kg)}
        zk = jnp.zeros((GDN_DK, dv), BF16)
        wst, qst = {}, {}
        for kh in range(kg):
            p = (ci, kh)
            sbd = jnp.concatenate([jnp.concatenate([stb[kh][:, :dv], zk], axis=1),
                                   jnp.concatenate([zk, stb[kh][:, dv:]], axis=1)], axis=0)
            wst[kh] = jnp.dot(ws[p], sbd, preferred_element_type=F32)
            qst[kh] = jnp.dot(qs[p], stb[kh], preferred_element_type=F32)
        for kh in range(kg):
            p = (ci, kh)
            vn = us[p] - wst[kh]
            vnb = vn.astype(BF16)
            vbd = jnp.concatenate([jnp.concatenate([vnb[:, :dv], zeros_v], axis=1),
                                   jnp.concatenate([zeros_v, vnb[:, dv:]], axis=1)], axis=0)
            av = jnp.dot(attn2s[p].astype(BF16), vbd, preferred_element_type=F32)
            ekl2 = jnp.concatenate(ekls[p], axis=1)
            svn = (vn * ekl2).astype(BF16)
            upd = lax.dot_general(ks[p], svn, (((0,), (0,)), ((), ())), preferred_element_type=F32)
            state_ref[kh] = st[kh] * cdecs[p] + upd
            egc2 = jnp.concatenate(egcs[p], axis=1)
            y2 = egc2 * qst[kh] + av
            for vi in range(2):
                hl = 2 * kh + vi
                y = y2[:, vi * dv:(vi + 1) * dv]
                ms = jnp.mean(y * y, axis=-1, keepdims=True)
                yn = y * lax.rsqrt(ms + RMS_EPS) * ng
                zz = z_ref[rows, hl * dv:(hl + 1) * dv].astype(F32)
                o_ref[rows, hl * dv:(hl + 1) * dv] = (yn * _silu(zz)).astype(o_ref.dtype)


def _gdn_core(q, k, v, qkvz, bg, bgt, norm_g):
    s = q.shape[0]
    kg, nc = GDN_KG, GDN_NC
    bt = nc * CHUNK
    kern = functools.partial(_gdn_core_kernel, kg=kg, nc=nc)
    vw = 2 * kg * GDN_DV
    z_off = GDN_QKV // vw
    return pl.pallas_call(
        kern,
        grid=(GDN_K_HEADS // kg, s // bt),
        in_specs=[pl.BlockSpec((bt, kg * GDN_DK), lambda g, t: (t, g)),
                  pl.BlockSpec((bt, kg * GDN_DK), lambda g, t: (t, g)),
                  pl.BlockSpec((bt, vw), lambda g, t: (t, g)),
                  pl.BlockSpec((bt, vw), lambda g, t: (t, g + z_off)),
                  pl.BlockSpec((None, bt, 4 * kg), lambda g, t: (g, t, 0)),
                  pl.BlockSpec((None, 4 * kg, bt), lambda g, t: (g, 0, t)),
                  pl.BlockSpec((1, GDN_DV), lambda g, t: (0, 0))],
        out_specs=pl.BlockSpec((bt, vw), lambda g, t: (t, g)),
        out_shape=jax.ShapeDtypeStruct((s, GDN_VW), BF16),
        scratch_shapes=[pltpu.VMEM((kg, GDN_DK, 2 * GDN_DV), F32)],
        compiler_params=_cparams(("parallel", "arbitrary")),
        name="gdn_core",
    )(q, k, v, qkvz, bg, bgt, norm_g)


def _retention_mixer(xb, xf, w_in, gn_g, w_out, ln_g, ln_b):
    s = xb.shape[0]
    half = RET_DK // 2
    inv = ROPE_BASE ** (-jnp.arange(half, dtype=F32) / half)
    ang = jnp.arange(s).astype(F32)[:, None] * inv[None, :]
    cos, sin = jnp.cos(ang), jnp.sin(ang)
    log_gamma = jnp.log1p(-jnp.exp2(-5.0 - jnp.arange(RET_HEADS, dtype=F32)))

    wb = w_in.astype(BF16)
    qk = _ret_qk_proj(xb, wb, cos, sin)
    vg = _matmul_cols(xb, wb, 2 * RET_QK, 2 * RET_VW, "ret_vg_proj")
    a = _ret_core(log_gamma, qk, vg, gn_g.reshape(1, RET_VW))
    return _out_proj_ln(a, w_out.astype(BF16), xf, ln_g.reshape(1, -1), ln_b.reshape(1, -1),
                        "ret_out_ln")


def _gdn_mixer(xb, xf, w_in, conv_w, a_log, dt_bias, norm_g, w_out, ln_g, ln_b):
    nh, gsz = GDN_V_HEADS, 2 * GDN_KG
    ng = nh // gsz
    wb = w_in[:, :GDN_QKV + GDN_VW].astype(BF16)
    w_b = w_in[:, GDN_QKV + GDN_VW:GDN_QKV + GDN_VW + nh].reshape(-1, ng, gsz)
    w_a = w_in[:, GDN_QKV + GDN_VW + nh:].reshape(-1, ng, gsz)
    w_ba = jnp.concatenate([w_b, w_a], axis=2).reshape(-1, 2 * nh).astype(BF16)
    zeros = jnp.zeros((ng, gsz), F32)

    def _lanes(vec):
        tab = jnp.concatenate([zeros, vec.astype(F32).reshape(ng, gsz)], axis=1).reshape(1, 2 * nh)
        return jnp.pad(tab, ((0, 0), (0, V7X_LANES - 2 * nh)))

    qkvz = _matmul_cols(xb, wb, 0, GDN_QKV + GDN_VW, "gdn_proj")
    bg, bgt = _gdn_beta_decay(xb, w_ba, _lanes(a_log), _lanes(dt_bias), gsz)
    q = _gdn_prep(qkvz, conv_w, 0, GDN_QK, True, GDN_DK ** -0.5, "gdn_prep_q")
    k = _gdn_prep(qkvz, conv_w, GDN_QK, GDN_QK, True, 1.0, "gdn_prep_k")
    v = _gdn_prep(qkvz, conv_w, 2 * GDN_QK, GDN_VW, False, 1.0, "gdn_prep_v")
    a = _gdn_core(q, k, v, qkvz, bg, bgt, norm_g.reshape(1, GDN_DV))
    return _out_proj_ln(a, w_out.astype(BF16), xf, ln_g.reshape(1, -1), ln_b.reshape(1, -1),
                        "gdn_out_ln")


def kernel(x, ret_w_in, ret_gn_g, ret_w_out, gdn_w_in, gdn_conv_w, gdn_a_log, gdn_dt_bias,
           gdn_norm_g, gdn_w_out, ln_mix_g, ln_mix_b, mlp_w1, mlp_w2, ln_ffn_g, ln_ffn_b):
    b, s, d = x.shape
    outs = []
    for bi in range(b):
        xf = x[bi]
        xb = xf.astype(BF16)
        for i in range(DEPTH):
            j = i // 2
            if i % 2 == 0:
                xf, xb = _retention_mixer(xb, xf, ret_w_in[j], ret_gn_g[j], ret_w_out[j],
                                          ln_mix_g[i], ln_mix_b[i])
            else:
                xf, xb = _gdn_mixer(xb, xf, gdn_w_in[j], gdn_conv_w[j], gdn_a_log[j],
                                    gdn_dt_bias[j], gdn_norm_g[j], gdn_w_out[j],
                                    ln_mix_g[i], ln_mix_b[i])
            xf, xb = _mlp_ln(xb, xf, mlp_w1[i].astype(BF16), mlp_w2[i].astype(BF16),
                             ln_ffn_g[i].reshape(1, -1), ln_ffn_b[i].reshape(1, -1), f"mlp_ln_{i}")
        outs.append(xf)
    return jnp.stack(outs, axis=0)
```

```python
import functools

import numpy as np
import jax
import jax.numpy as jnp
from jax import lax
from jax.experimental import pallas as pl
from jax.experimental.pallas import tpu as pltpu

F32 = jnp.float32
BF16 = jnp.bfloat16

D_MODEL = 2048
DEPTH = 2
CHUNK = 64

RET_HEADS = 8
RET_DK = D_MODEL // RET_HEADS
RET_DV = 2 * RET_DK
RET_QK = RET_HEADS * RET_DK
RET_VW = RET_HEADS * RET_DV
ROPE_BASE = 10000.0
GN_EPS = 1e-6

GDN_K_HEADS = 16
GDN_V_HEADS = 32
GDN_DK = 128
GDN_DV = 128
GDN_QK = GDN_K_HEADS * GDN_DK
GDN_VW = GDN_V_HEADS * GDN_DV
GDN_QKV = 2 * GDN_QK + GDN_VW
GDN_CONV = 4
RMS_EPS = 1e-6
L2_EPS = 1e-6

D_FF = 4 * D_MODEL
DN_ALPHA = (2.0 * DEPTH) ** 0.25
LN_EPS = 1e-5

V7X_LANES = 128
V7X_VMEM_LIMIT_BYTES = 56 * 1024 * 1024

MM_TM = 1024
MM_TN = 1024
LN_TM = 256
MLP_TM = 512
MLP_TF = 512
RET_BT = 256
RET_HPS = 2
GDN_PREP_TM = 512
GDN_PREP_TN = 1024
GDN_BA_TM = 512
GDN_KG = 8
GDN_NC = 2
GDN_BLK = 16


def _cparams(sem):
    return pltpu.CompilerParams(dimension_semantics=sem, vmem_limit_bytes=V7X_VMEM_LIMIT_BYTES)


def _layer_norm(y, g, b):
    mu = jnp.mean(y, axis=-1, keepdims=True)
    d = y - mu
    var = jnp.mean(d * d, axis=-1, keepdims=True)
    return d * lax.rsqrt(var + LN_EPS) * g + b


def _silu(x):
    return x * (1.0 / (1.0 + jnp.exp(-x)))


def _proj_kernel(*refs, rotary, n_side, n_q_tiles, k_scale, tn):
    it = iter(refs)
    x_ref, w_ref = next(it), next(it)
    if rotary:
        cos_ref, sin_ref = next(it), next(it)
    side_in = [next(it) for _ in range(n_side)]
    o_ref = next(it)
    side_out = [next(it) for _ in range(n_side)]
    wb_ref = next(it)

    @pl.when(pl.program_id(1) == 0)
    def _cast_w():
        wb_ref[...] = w_ref[...].astype(BF16)

    acc = jnp.dot(x_ref[...], wb_ref[...], preferred_element_type=F32)
    if rotary:
        scale = jnp.where(pl.program_id(0) >= n_q_tiles, k_scale, 1.0).astype(F32)
        cos = cos_ref[...] * scale
        sin = sin_ref[...] * scale
        half = RET_DK // 2
        for j in range(tn // RET_DK):
            t1 = acc[:, j * RET_DK: j * RET_DK + half]
            t2 = acc[:, j * RET_DK + half: (j + 1) * RET_DK]
            o_ref[:, j * RET_DK: j * RET_DK + half] = (t1 * cos - t2 * sin).astype(o_ref.dtype)
            o_ref[:, j * RET_DK + half: (j + 1) * RET_DK] = (t1 * sin + t2 * cos).astype(o_ref.dtype)
    else:
        o_ref[...] = acc.astype(o_ref.dtype)
    for si, so in zip(side_in, side_out):
        so[...] = si[...].astype(BF16)


def _proj(x, w, layer, col_off, ncols, name, rotary=None, side=()):
    m, k = x.shape
    tm, tn = min(MM_TM, m), MM_TN
    off = col_off // tn
    nn, nm = ncols // tn, m // tm
    steps = nn * nm
    in_specs = [pl.BlockSpec((tm, k), lambda n, i: (i, 0)),
                pl.BlockSpec((None, k, tn), lambda n, i: (layer, 0, n + off))]
    args = [x, w]
    if rotary is not None:
        half = RET_DK // 2
        in_specs += [pl.BlockSpec((tm, half), lambda n, i: (i, 0))] * 2
        args += list(rotary)
    out_specs = [pl.BlockSpec((tm, tn), lambda n, i: (i, n))]
    out_shape = [jax.ShapeDtypeStruct((m, ncols), BF16)]
    for arr, lyr in side:
        _, r, c = arr.shape
        nb = 1 << (min(steps, r // 16).bit_length() - 1)
        blk = lambda n, i, nb=nb: jnp.minimum(n * nm + i, nb - 1)
        in_specs.append(pl.BlockSpec((None, r // nb, c), lambda n, i, lyr=lyr, blk=blk: (lyr, blk(n, i), 0)))
        out_specs.append(pl.BlockSpec((r // nb, c), lambda n, i, blk=blk: (blk(n, i), 0)))
        out_shape.append(jax.ShapeDtypeStruct((r, c), BF16))
        args.append(arr)
    kern = functools.partial(_proj_kernel, rotary=rotary is not None, n_side=len(side),
                             n_q_tiles=RET_QK // tn, k_scale=RET_DK ** -0.5, tn=tn)
    return pl.pallas_call(
        kern,
        grid=(nn, nm),
        in_specs=in_specs,
        out_specs=out_specs,
        out_shape=out_shape,
        scratch_shapes=[pltpu.VMEM((k, tn), BF16)],
        compiler_params=_cparams(("arbitrary", "arbitrary")),
        name=name,
    )(*args)


def _ret_core_kernel(lg_ref, q_ref, k_ref, v_ref, gate_ref, gn_ref, o_ref,
                     state_ref, mask_ref, dq_ref, dk_ref, *, hps):
    t = pl.program_id(1)
    bt = q_ref.shape[0]

    @pl.when(t == 0)
    def _init():
        state_ref[...] = jnp.zeros_like(state_ref)
        r = lax.broadcasted_iota(jnp.int32, (bt, bt), 0)
        c = lax.broadcasted_iota(jnp.int32, (bt, bt), 1)
        dist = jnp.abs(r - c).astype(F32)
        visible = (c // CHUNK) <= (r // CHUNK)
        rr = lax.broadcasted_iota(jnp.int32, (bt, 1), 0).astype(F32)
        for hh in range(hps):
            lg = lg_ref[pl.program_id(0) * hps + hh]
            mask_ref[hh] = jnp.where(visible, jnp.exp(lg * dist), 0.0)
            dq_ref[hh] = jnp.exp(lg * (rr + 1.0))
            dk_ref[hh] = jnp.exp(lg * (bt - 1.0 - rr))

    for hh in range(hps):
        lg = lg_ref[pl.program_id(0) * hps + hh]
        q = q_ref[:, hh * RET_DK:(hh + 1) * RET_DK]
        k = k_ref[:, hh * RET_DK:(hh + 1) * RET_DK]
        v = v_ref[:, hh * RET_DV:(hh + 1) * RET_DV]
        s = lax.dot_general(q, k, (((1,), (1,)), ((), ())), preferred_element_type=F32)
        p = (s * mask_ref[hh]).astype(BF16)
        y = jnp.dot(p, v, preferred_element_type=F32)
        st = state_ref[hh]
        y = y + dq_ref[hh] * jnp.dot(q, st.astype(BF16), preferred_element_type=F32)
        kd = (k.astype(F32) * dk_ref[hh]).astype(BF16)
        upd = lax.dot_general(kd, v, (((0,), (0,)), ((), ())), preferred_element_type=F32)
        state_ref[hh] = st * jnp.exp(lg * bt) + upd

        mu = jnp.mean(y, axis=-1, keepdims=True)
        d = y - mu
        var = jnp.mean(d * d, axis=-1, keepdims=True)
        yn = d * lax.rsqrt(var + GN_EPS) * gn_ref[:, hh * RET_DV:(hh + 1) * RET_DV]
        gate = gate_ref[:, hh * RET_DV:(hh + 1) * RET_DV].astype(F32)
        o_ref[:, hh * RET_DV:(hh + 1) * RET_DV] = (_silu(gate) * yn).astype(o_ref.dtype)


def _ret_core(log_gamma, qk, vg, gn_g):
    s = qk.shape[0]
    bt = min(RET_BT, s)
    hps = RET_HPS
    nqb = RET_QK // (hps * RET_DK)
    nvb = RET_VW // (hps * RET_DV)
    grid_spec = pltpu.PrefetchScalarGridSpec(
        num_scalar_prefetch=1,
        grid=(RET_HEADS // hps, s // bt),
        in_specs=[pl.BlockSpec((bt, hps * RET_DK), lambda h, t, lg: (t, h)),
                  pl.BlockSpec((bt, hps * RET_DK), lambda h, t, lg: (t, nqb + h)),
                  pl.BlockSpec((bt, hps * RET_DV), lambda h, t, lg: (t, h)),
                  pl.BlockSpec((bt, hps * RET_DV), lambda h, t, lg: (t, nvb + h)),
                  pl.BlockSpec((1, hps * RET_DV), lambda h, t, lg: (0, h))],
        out_specs=pl.BlockSpec((bt, hps * RET_DV), lambda h, t, lg: (t, h)),
        scratch_shapes=[pltpu.VMEM((hps, RET_DK, RET_DV), F32),
                        pltpu.VMEM((hps, bt, bt), F32),
                        pltpu.VMEM((hps, bt, 1), F32),
                        pltpu.VMEM((hps, bt, 1), F32)],
    )
    return pl.pallas_call(
        functools.partial(_ret_core_kernel, hps=hps),
        grid_spec=grid_spec,
        out_shape=jax.ShapeDtypeStruct((s, RET_VW), BF16),
        compiler_params=_cparams(("parallel", "arbitrary")),
        name="ret_core",
    )(log_gamma, qk, qk, vg, vg, gn_g)


def _mm_ln_kernel(a_ref, w_ref, r_ref, g_ref, b_ref, of_ref, ob_ref):
    half = a_ref.shape[0] // 2
    for rows in (slice(0, half), slice(half, 2 * half)):
        acc = jnp.dot(a_ref[rows, :], w_ref[...], preferred_element_type=F32)
        out = _layer_norm(DN_ALPHA * r_ref[rows, :] + acc, g_ref[...], b_ref[...])
        of_ref[rows, :] = out
        ob_ref[rows, :] = out.astype(ob_ref.dtype)


def _out_proj_ln(a, w, resid, g, b, name):
    m, k = a.shape
    n = w.shape[1]
    tm = min(LN_TM, m)
    return pl.pallas_call(
        _mm_ln_kernel,
        grid=(m // tm,),
        in_specs=[pl.BlockSpec((tm, k), lambda i: (i, 0)),
                  pl.BlockSpec((k, n), lambda i: (0, 0), pipeline_mode=pl.Buffered(1)),
                  pl.BlockSpec((tm, n), lambda i: (i, 0)),
                  pl.BlockSpec((1, n), lambda i: (0, 0)),
                  pl.BlockSpec((1, n), lambda i: (0, 0))],
        out_specs=[pl.BlockSpec((tm, n), lambda i: (i, 0)),
                   pl.BlockSpec((tm, n), lambda i: (i, 0))],
        out_shape=[jax.ShapeDtypeStruct((m, n), F32), jax.ShapeDtypeStruct((m, n), BF16)],
        compiler_params=_cparams(("arbitrary",)),
        name=name,
    )(a, w, resid, g, b)


def _mlp_kernel(xb_ref, xf_ref, w1_ref, w2_ref, g_ref, b_ref, of_ref, ob_ref, acc_ref):
    f = pl.program_id(1)

    @pl.when(f == 0)
    def _first():
        acc_ref[...] = jnp.zeros_like(acc_ref)

    h = jnp.dot(xb_ref[...], w1_ref[...], preferred_element_type=F32)
    h = jnp.square(jnp.maximum(h, 0.0)).astype(BF16)
    acc_ref[...] += jnp.dot(h, w2_ref[...], preferred_element_type=F32)

    @pl.when(f == pl.num_programs(1) - 1)
    def _fin():
        out = _layer_norm(DN_ALPHA * xf_ref[...] + acc_ref[...], g_ref[...], b_ref[...])
        of_ref[...] = out
        ob_ref[...] = out.astype(ob_ref.dtype)


def _mlp_ln(xb, xf, w1, w2, g, b, name):
    m, d = xb.shape
    ff = w1.shape[1]
    tm, tf = min(MLP_TM, m), MLP_TF
    return pl.pallas_call(
        _mlp_kernel,
        grid=(m // tm, ff // tf),
        in_specs=[pl.BlockSpec((tm, d), lambda i, f: (i, 0)),
                  pl.BlockSpec((tm, d), lambda i, f: (i, 0)),
                  pl.BlockSpec((d, tf), lambda i, f: (0, f)),
                  pl.BlockSpec((tf, d), lambda i, f: (f, 0)),
                  pl.BlockSpec((1, d), lambda i, f: (0, 0)),
                  pl.BlockSpec((1, d), lambda i, f: (0, 0))],
        out_specs=[pl.BlockSpec((tm, d), lambda i, f: (i, 0)),
                   pl.BlockSpec((tm, d), lambda i, f: (i, 0))],
        out_shape=[jax.ShapeDtypeStruct((m, d), F32), jax.ShapeDtypeStruct((m, d), BF16)],
        scratch_shapes=[pltpu.VMEM((tm, d), F32)],
        compiler_params=_cparams(("parallel", "arbitrary")),
        name=name,
    )(xb, xf, w1, w2, g, b)


def _gdn_ba_kernel(x_ref, w_ref, alog_ref, dtb_ref, o_ref, ot_ref, *, gsz):
    acc = jnp.dot(x_ref[...], w_ref[...], preferred_element_type=F32)
    tm, nc = acc.shape
    acc = jnp.concatenate([acc, jnp.zeros((tm, V7X_LANES - nc), F32)], axis=1)
    lane = lax.broadcasted_iota(jnp.int32, acc.shape, 1)
    is_beta = (lane % (2 * gsz)) < gsz
    beta = 1.0 / (1.0 + jnp.exp(-acc))
    a = acc + dtb_ref[...]
    softplus = jnp.maximum(a, 0.0) + jnp.log(1.0 + jnp.exp(-jnp.abs(a)))
    g = -jnp.exp(alog_ref[...]) * softplus
    row = lax.broadcasted_iota(jnp.int32, acc.shape, 0) % CHUNK
    sh = 1
    while sh < CHUNK:
        g = g + jnp.where(row >= sh, pltpu.roll(g, sh, 0), 0.0)
        sh *= 2
    res = jnp.where(is_beta, beta, g)
    res_t = res.T
    for gi in range(nc // (2 * gsz)):
        o_ref[gi] = res[:, gi * 2 * gsz:(gi + 1) * 2 * gsz]
        ot_ref[gi] = res_t[gi * 2 * gsz:(gi + 1) * 2 * gsz, :]


def _gdn_beta_decay(xb, w_ba, a_log, dt_bias, gsz):
    m, k = xb.shape
    tm = min(GDN_BA_TM, m)
    nc = 2 * GDN_V_HEADS
    ng = GDN_V_HEADS // gsz
    kern = functools.partial(_gdn_ba_kernel, gsz=gsz)
    return pl.pallas_call(
        kern,
        grid=(m // tm,),
        in_specs=[pl.BlockSpec((tm, k), lambda i: (i, 0)),
                  pl.BlockSpec((k, nc), lambda i: (0, 0)),
                  pl.BlockSpec((1, V7X_LANES), lambda i: (0, 0)),
                  pl.BlockSpec((1, V7X_LANES), lambda i: (0, 0))],
        out_specs=[pl.BlockSpec((ng, tm, 2 * gsz), lambda i: (0, i, 0)),
                   pl.BlockSpec((ng, 2 * gsz, tm), lambda i: (0, 0, i))],
        out_shape=[jax.ShapeDtypeStruct((ng, m, 2 * gsz), F32),
                   jax.ShapeDtypeStruct((ng, 2 * gsz, m), F32)],
        compiler_params=_cparams(("parallel",)),
        name="gdn_beta_decay",
    )(xb, w_ba, a_log, dt_bias)


def _gdn_prep_kernel(cur_ref, prev_ref, cw_ref, o_ref, ext_ref, *, normalize, scale):
    i = pl.program_id(1)
    tm, tn = cur_ref.shape
    halo = prev_ref.shape[0]
    pad = 8
    prev = prev_ref[...].astype(F32)[halo - pad:, :]
    ext_ref[0:pad, :] = jnp.where(i > 0, prev, 0.0)
    ext_ref[pad:, :] = cur_ref[...].astype(F32)
    cw = cw_ref[...]
    acc = ext_ref[pad:, :] * cw[GDN_CONV - 1:GDN_CONV, :]
    for d in range(1, GDN_CONV):
        acc = acc + ext_ref[pad - d: pad - d + tm, :] * cw[GDN_CONV - 1 - d: GDN_CONV - d, :]
    y = _silu(acc)
    if normalize:
        for j in range(tn // GDN_DK):
            blk = y[:, j * GDN_DK:(j + 1) * GDN_DK]
            ss = jnp.sum(blk * blk, axis=-1, keepdims=True)
            o_ref[:, j * GDN_DK:(j + 1) * GDN_DK] = (
                blk * (lax.rsqrt(ss + L2_EPS) * scale)).astype(o_ref.dtype)
    else:
        o_ref[...] = y.astype(o_ref.dtype)


def _gdn_prep(qkv, conv_w, col_off, ncols, normalize, scale, name):
    s = qkv.shape[0]
    tm, tn = min(GDN_PREP_TM, s), GDN_PREP_TN
    halo = 16
    off = col_off // tn
    kern = functools.partial(_gdn_prep_kernel, normalize=normalize, scale=scale)
    return pl.pallas_call(
        kern,
        grid=(ncols // tn, s // tm),
        in_specs=[pl.BlockSpec((tm, tn), lambda n, i: (i, n + off)),
                  pl.BlockSpec((halo, tn),
                               lambda n, i: (jnp.maximum(i * (tm // halo) - 1, 0), n + off)),
                  pl.BlockSpec((GDN_CONV, tn), lambda n, i: (0, n + off))],
        out_specs=pl.BlockSpec((tm, tn), lambda n, i: (i, n)),
        out_shape=jax.ShapeDtypeStruct((s, ncols), BF16),
        scratch_shapes=[pltpu.VMEM((tm + 8, tn), F32)],
        compiler_params=_cparams(("parallel", "arbitrary")),
        name=name,
    )(qkv, qkv, conv_w)


def _pair_blockdiag(x2, left):
    top = jnp.where(left, x2, 0.0).astype(BF16)
    bot = jnp.where(left, 0.0, x2).astype(BF16)
    return jnp.concatenate([top, bot], axis=0)


def _gdn_core_kernel(q_ref, k_ref, v_ref, z_ref, bg_ref, bgt_ref, ng_ref, o_ref, state_ref,
                     *, kg, nc):
    t = pl.program_id(1)
    c = CHUNK
    gsz = 2 * kg
    dv = GDN_DV

    @pl.when(t == 0)
    def _init():
        state_ref[...] = jnp.zeros_like(state_ref)

    r = lax.broadcasted_iota(jnp.int32, (c, 2 * c), 0)
    lane = lax.broadcasted_iota(jnp.int32, (c, 2 * c), 1)
    left = lane < c
    col = lane & (c - 1)
    causal = r >= col
    strict = r > col
    eye2 = jnp.where(r == col, 1.0, 0.0).astype(F32)
    blk_mask = (r // GDN_BLK) == (col // GDN_BLK)
    ng = ng_ref[...]
    zeros_v = jnp.zeros((c, dv), BF16)

    bg_all = bg_ref[...]
    bgt_all = bgt_ref[...]

    pairs = [(ci, kh) for ci in range(nc) for kh in range(kg)]

    def pmm(x2, y2):
        return jnp.dot(x2.astype(BF16), _pair_blockdiag(y2, left), preferred_element_type=F32)

    qs, ks, a2s, attn2s, betas, egcs, ekls, cdecs = {}, {}, {}, {}, {}, {}, {}, {}
    for p in pairs:
        ci, kh = p
        rows = slice(ci * c, (ci + 1) * c)
        kcols = slice(kh * GDN_DK, (kh + 1) * GDN_DK)
        q = q_ref[rows, kcols]
        k = k_ref[rows, kcols]
        qs[p], ks[p] = q, k
        qk = jnp.concatenate([q, k], axis=0)
        kk2 = jnp.concatenate([k, k], axis=0)
        prod = lax.dot_general(qk, kk2, (((1,), (1,)), ((), ())), preferred_element_type=F32)
        qkt2, kkt2 = prod[:c], prod[c:]
        bcol, gcol, grow, glast = [], [], [], []
        for vi in range(2):
            hl = 2 * kh + vi
            bcol.append(jnp.broadcast_to(bg_all[rows, hl:hl + 1], (c, dv)))
            gcol.append(jnp.broadcast_to(bg_all[rows, gsz + hl:gsz + hl + 1], (c, dv)))
            grow.append(bgt_all[gsz + hl:gsz + hl + 1, rows])
            glast.append(gcol[vi][c - 1:c, :])
        beta2 = jnp.where(left, bcol[0], bcol[1])
        gc2 = jnp.where(left, gcol[0], gcol[1])
        grow2 = jnp.concatenate(grow, axis=1)
        dec2 = jnp.exp(jnp.where(causal, gc2 - grow2, -jnp.inf))
        a2s[p] = jnp.where(strict, beta2 * kkt2 * dec2, 0.0)
        attn2s[p] = qkt2 * dec2
        betas[p] = bcol
        egcs[p] = [jnp.exp(gcol[0]), jnp.exp(gcol[1])]
        ekls[p] = [jnp.exp(glast[0] - gcol[0]), jnp.exp(glast[1] - gcol[1])]
        cdecs[p] = jnp.concatenate([jnp.exp(glast[0]), jnp.exp(glast[1])], axis=1)

    ad = {p: jnp.where(blk_mask, a2s[p], 0.0) for p in pairs}
    ao = {p: a2s[p] - ad[p] for p in pairs}
    ima = {p: eye2 - ad[p] for p in pairs}
    x2 = {p: pmm(ad[p], ad[p]) for p in pairs}
    s1 = {p: pmm(jnp.concatenate([x2[p], ima[p]], axis=0), x2[p]) for p in pairs}
    x4 = {p: s1[p][:c] for p in pairs}
    p1 = {p: s1[p][c:] + ima[p] for p in pairs}
    s2 = {p: pmm(jnp.concatenate([x4[p], p1[p]], axis=0), x4[p]) for p in pairs}
    p2 = {p: s2[p][c:] + p1[p] for p in pairs}
    dinv = {p: pmm(p2[p], s2[p][:c]) + p2[p] for p in pairs}
    n1 = {p: pmm(dinv[p], ao[p]) for p in pairs}
    s3 = {p: jnp.dot(n1[p].astype(BF16),
                     jnp.concatenate([_pair_blockdiag(dinv[p], left),
                                      _pair_blockdiag(n1[p], left)], axis=1),
                     preferred_element_type=F32) for p in pairs}
    gm = {p: dinv[p] - s3[p][:, :2 * c] for p in pairs}
    tm = {p: gm[p] + pmm(s3[p][:, 2 * c:], gm[p]) for p in pairs}

    us, ws = {}, {}
    for p in pairs:
        ci, kh = p
        rows = slice(ci * c, (ci + 1) * c)
        kf = ks[p].astype(F32)
        tb = tm[p].astype(BF16)
        u2, w2 = [], []
        for vi in range(2):
            hl = 2 * kh + vi
            v = v_ref[rows, hl * dv:(hl + 1) * dv].astype(F32)
            beta = betas[p][vi]
            rhs = jnp.concatenate([v * beta, kf * (beta * egcs[p][vi])], axis=1).astype(BF16)
            uw = jnp.dot(tb[:, vi * c:(vi + 1) * c], rhs, preferred_element_type=F32)
            u2.append(uw[:, :dv])
            w2.append(uw[:, dv:])
        us[p] = jnp.concatenate(u2, axis=1)
        ws[p] = jnp.concatenate(w2, axis=1).astype(BF16)

    for ci in range(nc):
        rows = slice(ci * c, (ci + 1) * c)
        st = {kh: state_ref[kh] for kh in range(kg)}
        stb = {kh: st[kh].astype(BF16) for kh in range(kg)}
        zk = jnp.zeros((GDN_DK, dv), BF16)
        wst, qst = {}, {}
        for kh in range(kg):
            p = (ci, kh)
            sbd = jnp.concatenate([jnp.concatenate([stb[kh][:, :dv], zk], axis=1),
                                   jnp.concatenate([zk, stb[kh][:, dv:]], axis=1)], axis=0)
            wst[kh] = jnp.dot(ws[p], sbd, preferred_element_type=F32)
            qst[kh] = jnp.dot(qs[p], stb[kh], preferred_element_type=F32)
        for kh in range(kg):
            p = (ci, kh)
            vn = us[p] - wst[kh]
            vnb = vn.astype(BF16)
            vbd = jnp.concatenate([jnp.concatenate([vnb[:, :dv], zeros_v], axis=1),
                                   jnp.concatenate([zeros_v, vnb[:, dv:]], axis=1)], axis=0)
            av = jnp.dot(attn2s[p].astype(BF16), vbd, preferred_element_type=F32)
            ekl2 = jnp.concatenate(ekls[p], axis=1)
            svn = (vn * ekl2).astype(BF16)
            upd = lax.dot_general(ks[p], svn, (((0,), (0,)), ((), ())), preferred_element_type=F32)
            state_ref[kh] = st[kh] * cdecs[p] + upd
            egc2 = jnp.concatenate(egcs[p], axis=1)
            y2 = egc2 * qst[kh] + av
            for vi in range(2):
                hl = 2 * kh + vi
                y = y2[:, vi * dv:(vi + 1) * dv]
                ms = jnp.mean(y * y, axis=-1, keepdims=True)
                yn = y * lax.rsqrt(ms + RMS_EPS) * ng
                zz = z_ref[rows, hl * dv:(hl + 1) * dv].astype(F32)
                o_ref[rows, hl * dv:(hl + 1) * dv] = (yn * _silu(zz)).astype(o_ref.dtype)


def _gdn_core(q, k, v, qkvz, bg, bgt, norm_g):
    s = q.shape[0]
    kg, nc = GDN_KG, GDN_NC
    bt = nc * CHUNK
    kern = functools.partial(_gdn_core_kernel, kg=kg, nc=nc)
    vw = 2 * kg * GDN_DV
    z_off = GDN_QKV // vw
    return pl.pallas_call(
        kern,
        grid=(GDN_K_HEADS // kg, s // bt),
        in_specs=[pl.BlockSpec((bt, kg * GDN_DK), lambda g, t: (t, g)),
                  pl.BlockSpec((bt, kg * GDN_DK), lambda g, t: (t, g)),
                  pl.BlockSpec((bt, vw), lambda g, t: (t, g)),
                  pl.BlockSpec((bt, vw), lambda g, t: (t, g + z_off)),
                  pl.BlockSpec((None, bt, 4 * kg), lambda g, t: (g, t, 0)),
                  pl.BlockSpec((None, 4 * kg, bt), lambda g, t: (g, 0, t)),
                  pl.BlockSpec((1, GDN_DV), lambda g, t: (0, 0))],
        out_specs=pl.BlockSpec((bt, vw), lambda g, t: (t, g)),
        out_shape=jax.ShapeDtypeStruct((s, GDN_VW), BF16),
        scratch_shapes=[pltpu.VMEM((kg, GDN_DK, 2 * GDN_DV), F32)],
        compiler_params=_cparams(("parallel", "arbitrary")),
        name="gdn_core",
    )(q, k, v, qkvz, bg, bgt, norm_g)


def _rotary_tables(s):
    half = RET_DK // 2
    inv = np.float32(ROPE_BASE) ** (-np.arange(half, dtype=np.float32) / np.float32(half))
    ang = np.arange(s, dtype=np.float32)[:, None] * inv[None, :]
    return jnp.asarray(np.cos(ang), F32), jnp.asarray(np.sin(ang), F32)


def _retention_mixer(xb, xf, w_in, gn_g, w_out, mlp_w1, mlp_w2, layer, j, ln_g, ln_b):
    s = xb.shape[0]
    log_gamma = jnp.log1p(-jnp.exp2(-5.0 - jnp.arange(RET_HEADS, dtype=F32)))
    qk, w_out_b = _proj(xb, w_in, j, 0, 2 * RET_QK, "ret_qk_proj",
                        rotary=_rotary_tables(s), side=((w_out, j),))
    vg, w1b, w2b = _proj(xb, w_in, j, 2 * RET_QK, 2 * RET_VW, "ret_vg_proj",
                         side=((mlp_w1, layer), (mlp_w2, layer)))
    a = _ret_core(log_gamma, qk, vg, gn_g[j].reshape(1, RET_VW))
    xf, xb = _out_proj_ln(a, w_out_b, xf, ln_g.reshape(1, -1), ln_b.reshape(1, -1), "ret_out_ln")
    return xf, xb, w1b, w2b


def _gdn_mixer(xb, xf, w_in, conv_w, a_log, dt_bias, norm_g, w_out, mlp_w1, mlp_w2, layer, j,
               ln_g, ln_b):
    nh, gsz = GDN_V_HEADS, 2 * GDN_KG
    ng = nh // gsz
    w_tail = w_in[j][:, GDN_QKV + GDN_VW:]
    w_b = w_tail[:, :nh].reshape(-1, ng, gsz)
    w_a = w_tail[:, nh:].reshape(-1, ng, gsz)
    w_ba = jnp.concatenate([w_b, w_a], axis=2).reshape(-1, 2 * nh).astype(BF16)
    zeros = jnp.zeros((ng, gsz), F32)

    def _lanes(vec):
        tab = jnp.concatenate([zeros, vec.astype(F32).reshape(ng, gsz)], axis=1).reshape(1, 2 * nh)
        return jnp.pad(tab, ((0, 0), (0, V7X_LANES - 2 * nh)))

    qkvz, w_out_b, w1b, w2b = _proj(xb, w_in, j, 0, GDN_QKV + GDN_VW, "gdn_proj",
                                    side=((w_out, j), (mlp_w1, layer), (mlp_w2, layer)))
    bg, bgt = _gdn_beta_decay(xb, w_ba, _lanes(a_log[j]), _lanes(dt_bias[j]), gsz)
    cw = conv_w[j]
    q = _gdn_prep(qkvz, cw, 0, GDN_QK, True, GDN_DK ** -0.5, "gdn_prep_q")
    k = _gdn_prep(qkvz, cw, GDN_QK, GDN_QK, True, 1.0, "gdn_prep_k")
    v = _gdn_prep(qkvz, cw, 2 * GDN_QK, GDN_VW, False, 1.0, "gdn_prep_v")
    a = _gdn_core(q, k, v, qkvz, bg, bgt, norm_g[j].reshape(1, GDN_DV))
    xf, xb = _out_proj_ln(a, w_out_b, xf, ln_g.reshape(1, -1), ln_b.reshape(1, -1), "gdn_out_ln")
    return xf, xb, w1b, w2b


def kernel(x, ret_w_in, ret_gn_g, ret_w_out, gdn_w_in, gdn_conv_w, gdn_a_log, gdn_dt_bias,
           gdn_norm_g, gdn_w_out, ln_mix_g, ln_mix_b, mlp_w1, mlp_w2, ln_ffn_g, ln_ffn_b):
    b, s, d = x.shape
    outs = []
    for bi in range(b):
        xf = x.reshape(b * s, d) if b == 1 else x[bi]
        xb = xf.astype(BF16)
        for i in range(DEPTH):
            j = i // 2
            if i % 2 == 0:
                xf, xb, w1b, w2b = _retention_mixer(xb, xf, ret_w_in, ret_gn_g, ret_w_out,
                                                    mlp_w1, mlp_w2, i, j, ln_mix_g[i], ln_mix_b[i])
            else:
                xf, xb, w1b, w2b = _gdn_mixer(xb, xf, gdn_w_in, gdn_conv_w, gdn_a_log, gdn_dt_bias,
                                              gdn_norm_g, gdn_w_out, mlp_w1, mlp_w2, i, j,
                                              ln_mix_g[i], ln_mix_b[i])
            xf, xb = _mlp_ln(xb, xf, w1b, w2b, ln_ffn_g[i].reshape(1, -1),
                             ln_ffn_b[i].reshape(1, -1), f"mlp_ln_{i}")
        outs.append(xf)
    return outs[0].reshape(1, s, d) if b == 1 else jnp.stack(outs, axis=0)
```

```python
import functools

import numpy as np
import jax
import jax.numpy as jnp
from jax import lax
from jax.experimental import pallas as pl
from jax.experimental.pallas import tpu as pltpu

F32 = jnp.float32
BF16 = jnp.bfloat16

D_MODEL = 2048
DEPTH = 2
CHUNK = 64

RET_HEADS = 8
RET_DK = D_MODEL // RET_HEADS
RET_DV = 2 * RET_DK
RET_QK = RET_HEADS * RET_DK
RET_VW = RET_HEADS * RET_DV
ROPE_BASE = 10000.0
GN_EPS = 1e-6

GDN_K_HEADS = 16
GDN_V_HEADS = 32
GDN_DK = 128
GDN_DV = 128
GDN_QK = GDN_K_HEADS * GDN_DK
GDN_VW = GDN_V_HEADS * GDN_DV
GDN_QKV = 2 * GDN_QK + GDN_VW
GDN_CONV = 4
RMS_EPS = 1e-6
L2_EPS = 1e-6

D_FF = 4 * D_MODEL
DN_ALPHA = (2.0 * DEPTH) ** 0.25
LN_EPS = 1e-5

V7X_LANES = 128
V7X_VMEM_LIMIT_BYTES = 56 * 1024 * 1024

MM_TM = 1024
MM_TM_F32 = 512
MM_TN = 1024
LN_TM = 256
MLP_TM = 512
MLP_TF = 1024
RET_BT = 256
RET_HPS = 2
GDN_BA_TM = 512
GDN_KG = 8
GDN_NC = 2
GDN_BLK = 16


def _cparams(sem):
    return pltpu.CompilerParams(dimension_semantics=sem, vmem_limit_bytes=V7X_VMEM_LIMIT_BYTES)


def _layer_norm(y, g, b):
    mu = jnp.mean(y, axis=-1, keepdims=True)
    d = y - mu
    var = jnp.mean(d * d, axis=-1, keepdims=True)
    return d * lax.rsqrt(var + LN_EPS) * g + b


def _silu(x):
    return x * (1.0 / (1.0 + jnp.exp(-x)))


CONV_HALO = 8


def _proj_kernel(*refs, mode, n_side, n_first_tiles, first_scale, rest_scale, normalize, tn):
    it = iter(refs)
    x_ref, w_ref = next(it), next(it)
    if mode == "rotary":
        cos_ref, sin_ref = next(it), next(it)
    if mode == "conv":
        cw_ref = next(it)
    side_in = [next(it) for _ in range(n_side)]
    o_ref = next(it)
    side_out = [next(it) for _ in range(n_side)]
    wb_ref = next(it)
    if mode == "conv":
        ext_ref = next(it)
    i = pl.program_id(1)

    @pl.when(i == 0)
    def _cast_w():
        wb_ref[...] = w_ref[...].astype(BF16)
        if mode == "conv":
            ext_ref[...] = jnp.zeros((CONV_HALO, tn), F32)

    acc = jnp.dot(x_ref[...].astype(BF16), wb_ref[...], preferred_element_type=F32)
    scale = jnp.where(pl.program_id(0) < n_first_tiles, first_scale, rest_scale).astype(F32)
    if mode == "rotary":
        cos = cos_ref[...] * scale
        sin = sin_ref[...] * scale
        half = RET_DK // 2
        for j in range(tn // RET_DK):
            t1 = acc[:, j * RET_DK: j * RET_DK + half]
            t2 = acc[:, j * RET_DK + half: (j + 1) * RET_DK]
            o_ref[:, j * RET_DK: j * RET_DK + half] = (t1 * cos - t2 * sin).astype(o_ref.dtype)
            o_ref[:, j * RET_DK + half: (j + 1) * RET_DK] = (t1 * sin + t2 * cos).astype(o_ref.dtype)
    elif mode == "conv":
        tm = acc.shape[0]
        hist = ext_ref[...]
        for j in range(tn // GDN_DK):
            cols = slice(j * GDN_DK, (j + 1) * GDN_DK)
            a = acc[:, cols]
            cw = cw_ref[:, cols]
            y = a * cw[GDN_CONV - 1:GDN_CONV, :]
            top = jnp.concatenate([hist[:, cols], a[0:CONV_HALO]], axis=0)
            for d in range(1, GDN_CONV):
                first = pltpu.roll(top, d, 0)[CONV_HALO:]
                shifted = jnp.concatenate([first, pltpu.roll(a, d, 0)[CONV_HALO:]], axis=0)
                y = y + shifted * cw[GDN_CONV - 1 - d: GDN_CONV - d, :]
            y = _silu(y)
            if normalize:
                ss = jnp.sum(y * y, axis=-1, keepdims=True)
                y = y * (lax.rsqrt(ss + L2_EPS) * scale)
            o_ref[:, cols] = y.astype(o_ref.dtype)
        ext_ref[...] = acc[tm - CONV_HALO:, :]
    else:
        o_ref[...] = acc.astype(o_ref.dtype)
    for si, so in zip(side_in, side_out):
        so[...] = si[...].astype(BF16)


def _proj(x, w, layer, col_off, ncols, name, *, tm, mode="plain", rotary=None, conv_w=None,
          conv_layer=0, n_first_tiles=0, first_scale=1.0, rest_scale=1.0, normalize=False,
          side=()):
    m, k = x.shape
    tm, tn = min(tm, m), MM_TN
    off = col_off // tn
    nn, nm = ncols // tn, m // tm
    steps = nn * nm
    in_specs = [pl.BlockSpec((tm, k), lambda n, i: (i, 0)),
                pl.BlockSpec((None, k, tn), lambda n, i: (layer, 0, n + off))]
    args = [x, w]
    scratch = [pltpu.VMEM((k, tn), BF16)]
    if mode == "rotary":
        half = RET_DK // 2
        in_specs += [pl.BlockSpec((tm, half), lambda n, i: (i, 0))] * 2
        args += list(rotary)
    if mode == "conv":
        in_specs.append(pl.BlockSpec((None, GDN_CONV, tn), lambda n, i: (conv_layer, 0, n + off)))
        args.append(conv_w)
        scratch.append(pltpu.VMEM((CONV_HALO, tn), F32))
    out_specs = [pl.BlockSpec((tm, tn), lambda n, i: (i, n))]
    out_shape = [jax.ShapeDtypeStruct((m, ncols), BF16)]
    for arr, lyr in side:
        _, r, c = arr.shape
        nb = 1 << (min(steps, r // 16).bit_length() - 1)
        blk = lambda n, i, nb=nb: jnp.minimum(n * nm + i, nb - 1)
        in_specs.append(pl.BlockSpec((None, r // nb, c), lambda n, i, lyr=lyr, blk=blk: (lyr, blk(n, i), 0)))
        out_specs.append(pl.BlockSpec((r // nb, c), lambda n, i, blk=blk: (blk(n, i), 0)))
        out_shape.append(jax.ShapeDtypeStruct((r, c), BF16))
        args.append(arr)
    kern = functools.partial(_proj_kernel, mode=mode, n_side=len(side), n_first_tiles=n_first_tiles,
                             first_scale=first_scale, rest_scale=rest_scale, normalize=normalize,
                             tn=tn)
    return pl.pallas_call(
        kern,
        grid=(nn, nm),
        in_specs=in_specs,
        out_specs=out_specs,
        out_shape=out_shape,
        scratch_shapes=scratch,
        compiler_params=_cparams(("arbitrary", "arbitrary")),
        name=name,
    )(*args)


def _ret_core_kernel(lg_ref, q_ref, k_ref, v_ref, gate_ref, gn_ref, o_ref,
                     state_ref, mask_ref, dq_ref, dk_ref, *, hps):
    t = pl.program_id(1)
    bt = q_ref.shape[0]

    @pl.when(t == 0)
    def _init():
        state_ref[...] = jnp.zeros_like(state_ref)
        r = lax.broadcasted_iota(jnp.int32, (bt, bt), 0)
        c = lax.broadcasted_iota(jnp.int32, (bt, bt), 1)
        dist = jnp.abs(r - c).astype(F32)
        visible = (c // CHUNK) <= (r // CHUNK)
        rr = lax.broadcasted_iota(jnp.int32, (bt, 1), 0).astype(F32)
        for hh in range(hps):
            lg = lg_ref[pl.program_id(0) * hps + hh]
            mask_ref[hh] = jnp.where(visible, jnp.exp(lg * dist), 0.0)
            dq_ref[hh] = jnp.exp(lg * (rr + 1.0))
            dk_ref[hh] = jnp.exp(lg * (bt - 1.0 - rr))

    for hh in range(hps):
        lg = lg_ref[pl.program_id(0) * hps + hh]
        q = q_ref[:, hh * RET_DK:(hh + 1) * RET_DK]
        k = k_ref[:, hh * RET_DK:(hh + 1) * RET_DK]
        v = v_ref[:, hh * RET_DV:(hh + 1) * RET_DV]
        s = lax.dot_general(q, k, (((1,), (1,)), ((), ())), preferred_element_type=F32)
        p = (s * mask_ref[hh]).astype(BF16)
        y = jnp.dot(p, v, preferred_element_type=F32)
        st = state_ref[hh]
        y = y + dq_ref[hh] * jnp.dot(q, st.astype(BF16), preferred_element_type=F32)
        kd = (k.astype(F32) * dk_ref[hh]).astype(BF16)
        upd = lax.dot_general(kd, v, (((0,), (0,)), ((), ())), preferred_element_type=F32)
        state_ref[hh] = st * jnp.exp(lg * bt) + upd

        mu = jnp.mean(y, axis=-1, keepdims=True)
        d = y - mu
        var = jnp.mean(d * d, axis=-1, keepdims=True)
        yn = d * lax.rsqrt(var + GN_EPS) * gn_ref[:, hh * RET_DV:(hh + 1) * RET_DV]
        gate = gate_ref[:, hh * RET_DV:(hh + 1) * RET_DV].astype(F32)
        o_ref[:, hh * RET_DV:(hh + 1) * RET_DV] = (_silu(gate) * yn).astype(o_ref.dtype)


def _ret_core(log_gamma, qk, vg, gn_g):
    s = qk.shape[0]
    bt = min(RET_BT, s)
    hps = RET_HPS
    nqb = RET_QK // (hps * RET_DK)
    nvb = RET_VW // (hps * RET_DV)
    grid_spec = pltpu.PrefetchScalarGridSpec(
        num_scalar_prefetch=1,
        grid=(RET_HEADS // hps, s // bt),
        in_specs=[pl.BlockSpec((bt, hps * RET_DK), lambda h, t, lg: (t, h)),
                  pl.BlockSpec((bt, hps * RET_DK), lambda h, t, lg: (t, nqb + h)),
                  pl.BlockSpec((bt, hps * RET_DV), lambda h, t, lg: (t, h)),
                  pl.BlockSpec((bt, hps * RET_DV), lambda h, t, lg: (t, nvb + h)),
                  pl.BlockSpec((1, hps * RET_DV), lambda h, t, lg: (0, h))],
        out_specs=pl.BlockSpec((bt, hps * RET_DV), lambda h, t, lg: (t, h)),
        scratch_shapes=[pltpu.VMEM((hps, RET_DK, RET_DV), F32),
                        pltpu.VMEM((hps, bt, bt), F32),
                        pltpu.VMEM((hps, bt, 1), F32),
                        pltpu.VMEM((hps, bt, 1), F32)],
    )
    return pl.pallas_call(
        functools.partial(_ret_core_kernel, hps=hps),
        grid_spec=grid_spec,
        out_shape=jax.ShapeDtypeStruct((s, RET_VW), BF16),
        compiler_params=_cparams(("parallel", "arbitrary")),
        name="ret_core",
    )(log_gamma, qk, qk, vg, vg, gn_g)


def _mm_ln_kernel(a_ref, w_ref, r_ref, g_ref, b_ref, of_ref, ob_ref):
    half = a_ref.shape[0] // 2
    for rows in (slice(0, half), slice(half, 2 * half)):
        acc = jnp.dot(a_ref[rows, :], w_ref[...], preferred_element_type=F32)
        out = _layer_norm(DN_ALPHA * r_ref[rows, :] + acc, g_ref[...], b_ref[...])
        of_ref[rows, :] = out
        ob_ref[rows, :] = out.astype(ob_ref.dtype)


def _out_proj_ln(a, w, resid, g, b, name):
    m, k = a.shape
    n = w.shape[1]
    tm = min(LN_TM, m)
    return pl.pallas_call(
        _mm_ln_kernel,
        grid=(m // tm,),
        in_specs=[pl.BlockSpec((tm, k), lambda i: (i, 0)),
                  pl.BlockSpec((k, n), lambda i: (0, 0), pipeline_mode=pl.Buffered(1)),
                  pl.BlockSpec((tm, n), lambda i: (i, 0)),
                  pl.BlockSpec((1, n), lambda i: (0, 0)),
                  pl.BlockSpec((1, n), lambda i: (0, 0))],
        out_specs=[pl.BlockSpec((tm, n), lambda i: (i, 0)),
                   pl.BlockSpec((tm, n), lambda i: (i, 0))],
        out_shape=[jax.ShapeDtypeStruct((m, n), F32), jax.ShapeDtypeStruct((m, n), BF16)],
        compiler_params=_cparams(("arbitrary",)),
        name=name,
    )(a, w, resid, g, b)


def _mlp_kernel(xb_ref, xf_ref, w1_ref, w2_ref, g_ref, b_ref, of_ref, ob_ref, acc_ref):
    f = pl.program_id(1)

    @pl.when(f == 0)
    def _first():
        acc_ref[...] = jnp.zeros_like(acc_ref)

    h = jnp.dot(xb_ref[...], w1_ref[...], preferred_element_type=F32)
    h = jnp.square(jnp.maximum(h, 0.0)).astype(BF16)
    acc_ref[...] += jnp.dot(h, w2_ref[...], preferred_element_type=F32)

    @pl.when(f == pl.num_programs(1) - 1)
    def _fin():
        out = _layer_norm(DN_ALPHA * xf_ref[...] + acc_ref[...], g_ref[...], b_ref[...])
        of_ref[...] = out
        ob_ref[...] = out.astype(ob_ref.dtype)


def _mlp_ln(xb, xf, w1, w2, g, b, name):
    m, d = xb.shape
    ff = w1.shape[1]
    tm, tf = min(MLP_TM, m), MLP_TF
    return pl.pallas_call(
        _mlp_kernel,
        grid=(m // tm, ff // tf),
        in_specs=[pl.BlockSpec((tm, d), lambda i, f: (i, 0)),
                  pl.BlockSpec((tm, d), lambda i, f: (i, 0)),
                  pl.BlockSpec((d, tf), lambda i, f: (0, f)),
                  pl.BlockSpec((tf, d), lambda i, f: (f, 0)),
                  pl.BlockSpec((1, d), lambda i, f: (0, 0)),
                  pl.BlockSpec((1, d), lambda i, f: (0, 0))],
        out_specs=[pl.BlockSpec((tm, d), lambda i, f: (i, 0)),
                   pl.BlockSpec((tm, d), lambda i, f: (i, 0))],
        out_shape=[jax.ShapeDtypeStruct((m, d), F32), jax.ShapeDtypeStruct((m, d), BF16)],
        scratch_shapes=[pltpu.VMEM((tm, d), F32)],
        compiler_params=_cparams(("parallel", "arbitrary")),
        name=name,
    )(xb, xf, w1, w2, g, b)


def _gdn_ba_kernel(x_ref, w_ref, alog_ref, dtb_ref, o_ref, ot_ref, *, gsz):
    acc = jnp.dot(x_ref[...].astype(BF16), w_ref[...], preferred_element_type=F32)
    tm, nc = acc.shape
    acc = jnp.concatenate([acc, jnp.zeros((tm, V7X_LANES - nc), F32)], axis=1)
    lane = lax.broadcasted_iota(jnp.int32, acc.shape, 1)
    is_beta = (lane % (2 * gsz)) < gsz
    beta = 1.0 / (1.0 + jnp.exp(-acc))
    a = acc + dtb_ref[...]
    softplus = jnp.maximum(a, 0.0) + jnp.log(1.0 + jnp.exp(-jnp.abs(a)))
    g = -jnp.exp(alog_ref[...]) * softplus
    row = lax.broadcasted_iota(jnp.int32, acc.shape, 0) % CHUNK
    sh = 1
    while sh < CHUNK:
        g = g + jnp.where(row >= sh, pltpu.roll(g, sh, 0), 0.0)
        sh *= 2
    res = jnp.where(is_beta, beta, g)
    res_t = res.T
    for gi in range(nc // (2 * gsz)):
        o_ref[gi] = res[:, gi * 2 * gsz:(gi + 1) * 2 * gsz]
        ot_ref[gi] = res_t[gi * 2 * gsz:(gi + 1) * 2 * gsz, :]


def _gdn_beta_decay(xb, w_ba, a_log, dt_bias, gsz):
    m, k = xb.shape
    tm = min(GDN_BA_TM, m)
    nc = 2 * GDN_V_HEADS
    ng = GDN_V_HEADS // gsz
    kern = functools.partial(_gdn_ba_kernel, gsz=gsz)
    return pl.pallas_call(
        kern,
        grid=(m // tm,),
        in_specs=[pl.BlockSpec((tm, k), lambda i: (i, 0)),
                  pl.BlockSpec((k, nc), lambda i: (0, 0)),
                  pl.BlockSpec((1, V7X_LANES), lambda i: (0, 0)),
                  pl.BlockSpec((1, V7X_LANES), lambda i: (0, 0))],
        out_specs=[pl.BlockSpec((ng, tm, 2 * gsz), lambda i: (0, i, 0)),
                   pl.BlockSpec((ng, 2 * gsz, tm), lambda i: (0, 0, i))],
        out_shape=[jax.ShapeDtypeStruct((ng, m, 2 * gsz), F32),
                   jax.ShapeDtypeStruct((ng, 2 * gsz, m), F32)],
        compiler_params=_cparams(("parallel",)),
        name="gdn_beta_decay",
    )(xb, w_ba, a_log, dt_bias)


def _pair_blockdiag(x2, left):
    top = jnp.where(left, x2, 0.0).astype(BF16)
    bot = jnp.where(left, 0.0, x2).astype(BF16)
    return jnp.concatenate([top, bot], axis=0)


def _gdn_core_kernel(q_ref, k_ref, v_ref, z_ref, bg_ref, bgt_ref, ng_ref, o_ref, state_ref,
                     *, kg, nc):
    t = pl.program_id(1)
    c = CHUNK
    gsz = 2 * kg
    dv = GDN_DV

    @pl.when(t == 0)
    def _init():
        state_ref[...] = jnp.zeros_like(state_ref)

    r = lax.broadcasted_iota(jnp.int32, (c, 2 * c), 0)
    lane = lax.broadcasted_iota(jnp.int32, (c, 2 * c), 1)
    left = lane < c
    col = lane & (c - 1)
    causal = r >= col
    strict = r > col
    eye2 = jnp.where(r == col, 1.0, 0.0).astype(F32)
    blk_mask = (r // GDN_BLK) == (col // GDN_BLK)
    ng = ng_ref[...]
    zeros_v = jnp.zeros((c, dv), BF16)

    bg_all = bg_ref[...]
    bgt_all = bgt_ref[...]

    pairs = [(ci, kh) for ci in range(nc) for kh in range(kg)]

    def pmm(x2, y2):
        return jnp.dot(x2.astype(BF16), _pair_blockdiag(y2, left), preferred_element_type=F32)

    qs, ks, a2s, attn2s, betas, egcs, ekls, cdecs = {}, {}, {}, {}, {}, {}, {}, {}
    for p in pairs:
        ci, kh = p
        rows = slice(ci * c, (ci + 1) * c)
        kcols = slice(kh * GDN_DK, (kh + 1) * GDN_DK)
        q = q_ref[rows, kcols]
        k = k_ref[rows, kcols]
        qs[p], ks[p] = q, k
        qk = jnp.concatenate([q, k], axis=0)
        kk2 = jnp.concatenate([k, k], axis=0)
        prod = lax.dot_general(qk, kk2, (((1,), (1,)), ((), ())), preferred_element_type=F32)
        qkt2, kkt2 = prod[:c], prod[c:]
        bcol, gcol, grow, glast = [], [], [], []
        for vi in range(2):
            hl = 2 * kh + vi
            bcol.append(jnp.broadcast_to(bg_all[rows, hl:hl + 1], (c, dv)))
            gcol.append(jnp.broadcast_to(bg_all[rows, gsz + hl:gsz + hl + 1], (c, dv)))
            grow.append(bgt_all[gsz + hl:gsz + hl + 1, rows])
            glast.append(gcol[vi][c - 1:c, :])
        beta2 = jnp.where(left, bcol[0], bcol[1])
        gc2 = jnp.where(left, gcol[0], gcol[1])
        grow2 = jnp.concatenate(grow, axis=1)
        dec2 = jnp.exp(jnp.where(causal, gc2 - grow2, -jnp.inf))
        a2s[p] = jnp.where(strict, beta2 * kkt2 * dec2, 0.0)
        attn2s[p] = qkt2 * dec2
        betas[p] = bcol
        egcs[p] = [jnp.exp(gcol[0]), jnp.exp(gcol[1])]
        ekls[p] = [jnp.exp(glast[0] - gcol[0]), jnp.exp(glast[1] - gcol[1])]
        cdecs[p] = jnp.concatenate([jnp.exp(glast[0]), jnp.exp(glast[1])], axis=1)

    ad = {p: jnp.where(blk_mask, a2s[p], 0.0) for p in pairs}
    ao = {p: a2s[p] - ad[p] for p in pairs}
    ima = {p: eye2 - ad[p] for p in pairs}
    x2 = {p: pmm(ad[p], ad[p]) for p in pairs}
    s1 = {p: pmm(jnp.concatenate([x2[p], ima[p]], axis=0), x2[p]) for p in pairs}
    x4 = {p: s1[p][:c] for p in pairs}
    p1 = {p: s1[p][c:] + ima[p] for p in pairs}
    s2 = {p: pmm(jnp.concatenate([x4[p], p1[p]], axis=0), x4[p]) for p in pairs}
    p2 = {p: s2[p][c:] + p1[p] for p in pairs}
    dinv = {p: pmm(p2[p], s2[p][:c]) + p2[p] for p in pairs}
    n1 = {p: pmm(dinv[p], ao[p]) for p in pairs}
    s3 = {p: jnp.dot(n1[p].astype(BF16),
                     jnp.concatenate([_pair_blockdiag(dinv[p], left),
                                      _pair_blockdiag(n1[p], left)], axis=1),
                     preferred_element_type=F32) for p in pairs}
    gm = {p: dinv[p] - s3[p][:, :2 * c] for p in pairs}
    tm = {p: gm[p] + pmm(s3[p][:, 2 * c:], gm[p]) for p in pairs}

    us, ws = {}, {}
    for p in pairs:
        ci, kh = p
        rows = slice(ci * c, (ci + 1) * c)
        kf = ks[p].astype(F32)
        tb = tm[p].astype(BF16)
        u2, w2 = [], []
        for vi in range(2):
            hl = 2 * kh + vi
            v = v_ref[rows, hl * dv:(hl + 1) * dv].astype(F32)
            beta = betas[p][vi]
            rhs = jnp.concatenate([v * beta, kf * (beta * egcs[p][vi])], axis=1).astype(BF16)
            uw = jnp.dot(tb[:, vi * c:(vi + 1) * c], rhs, preferred_element_type=F32)
            u2.append(uw[:, :dv])
            w2.append(uw[:, dv:])
        us[p] = jnp.concatenate(u2, axis=1)
        ws[p] = jnp.concatenate(w2, axis=1).astype(BF16)

    for ci in range(nc):
        rows = slice(ci * c, (ci + 1) * c)
        st = {kh: state_ref[kh] for kh in range(kg)}
        stb = {kh: st[kh].astype(BF16) for kh in range(kg)}
        zk = jnp.zeros((GDN_DK, dv), BF16)
        wst, qst = {}, {}
        for kh in range(kg):
            p = (ci, kh)
            sbd = jnp.concatenate([jnp.concatenate([stb[kh][:, :dv], zk], axis=1),
                                   jnp.concatenate([zk, stb[kh][:, dv:]], axis=1)], axis=0)
            wst[kh] = jnp.dot(ws[p], sbd, preferred_element_type=F32)
            qst[kh] = jnp.dot(qs[p], stb[kh], preferred_element_type=F32)
        for kh in range(kg):
            p = (ci, kh)
            vn = us[p] - wst[kh]
            vnb = vn.astype(BF16)
            vbd = jnp.concatenate([jnp.concatenate([vnb[:, :dv], zeros_v], axis=1),
                                   jnp.concatenate([zeros_v, vnb[:, dv:]], axis=1)], axis=0)
            av = jnp.dot(attn2s[p].astype(BF16), vbd, preferred_element_type=F32)
            ekl2 = jnp.concatenate(ekls[p], axis=1)
            svn = (vn * ekl2).astype(BF16)
            upd = lax.dot_general(ks[p], svn, (((0,), (0,)), ((), ())), preferred_element_type=F32)
            state_ref[kh] = st[kh] * cdecs[p] + upd
            egc2 = jnp.concatenate(egcs[p], axis=1)
            y2 = egc2 * qst[kh] + av
            for vi in range(2):
                hl = 2 * kh + vi
                y = y2[:, vi * dv:(vi + 1) * dv]
                ms = jnp.mean(y * y, axis=-1, keepdims=True)
                yn = y * lax.rsqrt(ms + RMS_EPS) * ng
                zz = z_ref[rows, hl * dv:(hl + 1) * dv].astype(F32)
                o_ref[rows, hl * dv:(hl + 1) * dv] = (yn * _silu(zz)).astype(o_ref.dtype)


def _gdn_core(qk, v, z, bg, bgt, norm_g):
    s = qk.shape[0]
    kg, nc = GDN_KG, GDN_NC
    bt = nc * CHUNK
    kern = functools.partial(_gdn_core_kernel, kg=kg, nc=nc)
    vw = 2 * kg * GDN_DV
    k_off = GDN_K_HEADS // kg
    return pl.pallas_call(
        kern,
        grid=(GDN_K_HEADS // kg, s // bt),
        in_specs=[pl.BlockSpec((bt, kg * GDN_DK), lambda g, t: (t, g)),
                  pl.BlockSpec((bt, kg * GDN_DK), lambda g, t: (t, g + k_off)),
                  pl.BlockSpec((bt, vw), lambda g, t: (t, g)),
                  pl.BlockSpec((bt, vw), lambda g, t: (t, g)),
                  pl.BlockSpec((None, bt, 4 * kg), lambda g, t: (g, t, 0)),
                  pl.BlockSpec((None, 4 * kg, bt), lambda g, t: (g, 0, t)),
                  pl.BlockSpec((1, GDN_DV), lambda g, t: (0, 0))],
        out_specs=pl.BlockSpec((bt, vw), lambda g, t: (t, g)),
        out_shape=jax.ShapeDtypeStruct((s, GDN_VW), BF16),
        scratch_shapes=[pltpu.VMEM((kg, GDN_DK, 2 * GDN_DV), F32)],
        compiler_params=_cparams(("parallel", "arbitrary")),
        name="gdn_core",
    )(qk, qk, v, z, bg, bgt, norm_g)


def _rotary_tables(s):
    half = RET_DK // 2
    inv = np.float32(ROPE_BASE) ** (-np.arange(half, dtype=np.float32) / np.float32(half))
    ang = np.arange(s, dtype=np.float32)[:, None] * inv[None, :]
    return jnp.asarray(np.cos(ang), F32), jnp.asarray(np.sin(ang), F32)


def _retention_mixer(xin, xf, w_in, gn_g, w_out, mlp_w1, mlp_w2, layer, j, ln_g, ln_b):
    s = xin.shape[0]
    tm = MM_TM if xin.dtype == BF16 else MM_TM_F32
    log_gamma = jnp.log1p(-jnp.exp2(-5.0 - jnp.arange(RET_HEADS, dtype=F32)))
    qk, w_out_b = _proj(xin, w_in, j, 0, 2 * RET_QK, "ret_qk_proj", tm=tm, mode="rotary",
                        rotary=_rotary_tables(s), n_first_tiles=RET_QK // MM_TN,
                        first_scale=1.0, rest_scale=RET_DK ** -0.5,
                        side=((w_out, j),))
    vg, w1b, w2b = _proj(xin, w_in, j, 2 * RET_QK, 2 * RET_VW, "ret_vg_proj", tm=tm,
                         side=((mlp_w1, layer), (mlp_w2, layer)))
    a = _ret_core(log_gamma, qk, vg, gn_g[j].reshape(1, RET_VW))
    xf, xb = _out_proj_ln(a, w_out_b, xf, ln_g.reshape(1, -1), ln_b.reshape(1, -1), "ret_out_ln")
    return xf, xb, w1b, w2b


def _gdn_mixer(xin, xf, w_in, conv_w, a_log, dt_bias, norm_g, w_out, mlp_w1, mlp_w2, layer, j,
               ln_g, ln_b):
    nh, gsz = GDN_V_HEADS, 2 * GDN_KG
    ng = nh // gsz
    tm = MM_TM if xin.dtype == BF16 else MM_TM_F32
    col0 = GDN_QKV + GDN_VW
    w_tail = lax.slice(w_in, (j, 0, col0), (j + 1, w_in.shape[1], col0 + 2 * nh))[0]
    w_b = w_tail[:, :nh].reshape(-1, ng, gsz)
    w_a = w_tail[:, nh:].reshape(-1, ng, gsz)
    w_ba = jnp.concatenate([w_b, w_a], axis=2).reshape(-1, 2 * nh).astype(BF16)
    zeros = jnp.zeros((ng, gsz), F32)

    def _lanes(vec):
        tab = jnp.concatenate([zeros, vec.astype(F32).reshape(ng, gsz)], axis=1).reshape(1, 2 * nh)
        return jnp.pad(tab, ((0, 0), (0, V7X_LANES - 2 * nh)))

    qk, w_out_b = _proj(xin, w_in, j, 0, 2 * GDN_QK, "gdn_qk_proj", tm=tm, mode="conv",
                        conv_w=conv_w, conv_layer=j, n_first_tiles=GDN_QK // MM_TN,
                        first_scale=GDN_DK ** -0.5, rest_scale=1.0, normalize=True,
                        side=((w_out, j),))
    v, w1b = _proj(xin, w_in, j, 2 * GDN_QK, GDN_VW, "gdn_v_proj", tm=tm, mode="conv",
                   conv_w=conv_w, conv_layer=j, side=((mlp_w1, layer),))
    z, w2b = _proj(xin, w_in, j, GDN_QKV, GDN_VW, "gdn_z_proj", tm=tm, side=((mlp_w2, layer),))
    bg, bgt = _gdn_beta_decay(xin, w_ba, _lanes(a_log[j]), _lanes(dt_bias[j]), gsz)
    a = _gdn_core(qk, v, z, bg, bgt, norm_g[j].reshape(1, GDN_DV))
    xf, xb = _out_proj_ln(a, w_out_b, xf, ln_g.reshape(1, -1), ln_b.reshape(1, -1), "gdn_out_ln")
    return xf, xb, w1b, w2b


def kernel(x, ret_w_in, ret_gn_g, ret_w_out, gdn_w_in, gdn_conv_w, gdn_a_log, gdn_dt_bias,
           gdn_norm_g, gdn_w_out, ln_mix_g, ln_mix_b, mlp_w1, mlp_w2, ln_ffn_g, ln_ffn_b):
    b, s, d = x.shape
    outs = []
    for bi in range(b):
        xf = x.reshape(b * s, d) if b == 1 else x[bi]
        xb = xf
        for i in range(DEPTH):
            j = i // 2
            if i % 2 == 0:
                xf, xb, w1b, w2b = _retention_mixer(xb, xf, ret_w_in, ret_gn_g, ret_w_out,
                                                    mlp_w1, mlp_w2, i, j, ln_mix_g[i], ln_mix_b[i])
            else:
                xf, xb, w1b, w2b = _gdn_mixer(xb, xf, gdn_w_in, gdn_conv_w, gdn_a_log, gdn_dt_bias,
                                              gdn_norm_g, gdn_w_out, mlp_w1, mlp_w2, i, j,
                                              ln_mix_g[i], ln_mix_b[i])
            xf, xb = _mlp_ln(xb, xf, w1b, w2b, ln_ffn_g[i].reshape(1, -1),
                             ln_ffn_b[i].reshape(1, -1), f"mlp_ln_{i}")
        outs.append(xf)
    return outs[0].reshape(1, s, d) if b == 1 else jnp.stack(outs, axis=0)
```

```python
import functools

import numpy as np
import jax
import jax.numpy as jnp
from jax import lax
from jax.experimental import pallas as pl
from jax.experimental.pallas import tpu as pltpu

F32 = jnp.float32
BF16 = jnp.bfloat16

D_MODEL = 2048
DEPTH = 2
CHUNK = 64

RET_HEADS = 8
RET_DK = D_MODEL // RET_HEADS
RET_DV = 2 * RET_DK
RET_QK = RET_HEADS * RET_DK
RET_VW = RET_HEADS * RET_DV
ROPE_BASE = 10000.0
GN_EPS = 1e-6

GDN_K_HEADS = 16
GDN_V_HEADS = 32
GDN_DK = 128
GDN_DV = 128
GDN_QK = GDN_K_HEADS * GDN_DK
GDN_VW = GDN_V_HEADS * GDN_DV
GDN_QKV = 2 * GDN_QK + GDN_VW
GDN_CONV = 4
RMS_EPS = 1e-6
L2_EPS = 1e-6

D_FF = 4 * D_MODEL
DN_ALPHA = (2.0 * DEPTH) ** 0.25
LN_EPS = 1e-5

V7X_LANES = 128
V7X_VMEM_LIMIT_BYTES = 56 * 1024 * 1024

MM_TM = 1024
MM_TM_F32 = 512
MM_TN = 1024
LN_TM = 256
MLP_TM = 512
MLP_TF = 1024
RET_BT = 256
RET_HPS = 4
GDN_BA_TM = 512
GDN_KG = 8
GDN_NC = 2
GDN_BLK = 16


def _cparams(sem):
    return pltpu.CompilerParams(dimension_semantics=sem, vmem_limit_bytes=V7X_VMEM_LIMIT_BYTES)


def _layer_norm(y, g, b):
    mu = jnp.mean(y, axis=-1, keepdims=True)
    d = y - mu
    var = jnp.mean(d * d, axis=-1, keepdims=True)
    return d * lax.rsqrt(var + LN_EPS) * g + b


def _silu(x):
    return x * (1.0 / (1.0 + jnp.exp(-x)))


CONV_HALO = 8


def _proj_kernel(*refs, mode, n_side, n_first_tiles, first_scale, rest_scale, normalize, tn, w_t):
    it = iter(refs)
    x_ref, w_ref = next(it), next(it)
    if mode == "rotary":
        cos_ref, sin_ref = next(it), next(it)
    if mode == "conv":
        cw_ref = next(it)
    side_in = [next(it) for _ in range(n_side)]
    o_ref = next(it)
    side_out = [next(it) for _ in range(n_side)]
    wb_ref = next(it)
    if mode == "conv":
        ext_ref = next(it)
    i = pl.program_id(1)

    @pl.when(i == 0)
    def _cast_w():
        wb_ref[...] = w_ref[...].astype(BF16)
        if mode == "conv":
            ext_ref[...] = jnp.zeros((CONV_HALO, tn), F32)

    if w_t:
        acc = lax.dot_general(x_ref[...].astype(BF16), wb_ref[...], (((1,), (1,)), ((), ())),
                              preferred_element_type=F32)
    else:
        acc = jnp.dot(x_ref[...].astype(BF16), wb_ref[...], preferred_element_type=F32)
    scale = jnp.where(pl.program_id(0) < n_first_tiles, first_scale, rest_scale).astype(F32)
    if mode == "rotary":
        cos = cos_ref[...] * scale
        sin = sin_ref[...] * scale
        half = RET_DK // 2
        for j in range(tn // RET_DK):
            t1 = acc[:, j * RET_DK: j * RET_DK + half]
            t2 = acc[:, j * RET_DK + half: (j + 1) * RET_DK]
            o_ref[:, j * RET_DK: j * RET_DK + half] = (t1 * cos - t2 * sin).astype(o_ref.dtype)
            o_ref[:, j * RET_DK + half: (j + 1) * RET_DK] = (t1 * sin + t2 * cos).astype(o_ref.dtype)
    elif mode == "conv":
        tm = acc.shape[0]
        hist = ext_ref[...]
        nt = tm // CONV_HALO
        sub = lax.broadcasted_iota(jnp.int32, (1, CONV_HALO, GDN_DK), 1)
        for j in range(tn // GDN_DK):
            cols = slice(j * GDN_DK, (j + 1) * GDN_DK)
            a = acc[:, cols]
            cw = cw_ref[:, cols]
            y = a * cw[GDN_CONV - 1:GDN_CONV, :]
            ext = jnp.concatenate([hist[:, cols].reshape(1, CONV_HALO, GDN_DK),
                                   a.reshape(nt, CONV_HALO, GDN_DK)], axis=0)
            for d in range(1, GDN_CONV):
                r = pltpu.roll(ext, d, 1)
                shifted = jnp.where(sub < d, r[:-1], r[1:]).reshape(tm, GDN_DK)
                y = y + shifted * cw[GDN_CONV - 1 - d: GDN_CONV - d, :]
            y = _silu(y)
            if normalize:
                ss = jnp.sum(y * y, axis=-1, keepdims=True)
                y = y * (lax.rsqrt(ss + L2_EPS) * scale)
            o_ref[:, cols] = y.astype(o_ref.dtype)
        ext_ref[...] = acc[tm - CONV_HALO:, :]
    else:
        o_ref[...] = acc.astype(o_ref.dtype)
    for si, so in zip(side_in, side_out):
        so[...] = si[...].astype(BF16)


def _proj(x, w, layer, col_off, ncols, name, *, tm, w_t=False, mode="plain", rotary=None,
          conv_w=None, conv_layer=0, n_first_tiles=0, first_scale=1.0, rest_scale=1.0,
          normalize=False, side=()):
    m, k = x.shape
    tm, tn = min(tm, m), MM_TN
    off = col_off // tn
    nn, nm = ncols // tn, m // tm
    steps = nn * nm
    if w_t:
        w_spec = pl.BlockSpec((None, tn, k), lambda n, i: (layer, n + off, 0))
        scratch = [pltpu.VMEM((tn, k), BF16)]
    else:
        w_spec = pl.BlockSpec((None, k, tn), lambda n, i: (layer, 0, n + off))
        scratch = [pltpu.VMEM((k, tn), BF16)]
    in_specs = [pl.BlockSpec((tm, k), lambda n, i: (i, 0)), w_spec]
    args = [x, w]
    if mode == "rotary":
        half = RET_DK // 2
        in_specs += [pl.BlockSpec((tm, half), lambda n, i: (i, 0))] * 2
        args += list(rotary)
    if mode == "conv":
        in_specs.append(pl.BlockSpec((None, GDN_CONV, tn), lambda n, i: (conv_layer, 0, n + off)))
        args.append(conv_w)
        scratch.append(pltpu.VMEM((CONV_HALO, tn), F32))
    out_specs = [pl.BlockSpec((tm, tn), lambda n, i: (i, n))]
    out_shape = [jax.ShapeDtypeStruct((m, ncols), BF16)]
    for arr, lyr in side:
        _, r, c = arr.shape
        nb = 1 << (min(steps, r // 16).bit_length() - 1)
        blk = lambda n, i, nb=nb: jnp.minimum(n * nm + i, nb - 1)
        in_specs.append(pl.BlockSpec((None, r // nb, c), lambda n, i, lyr=lyr, blk=blk: (lyr, blk(n, i), 0)))
        out_specs.append(pl.BlockSpec((r // nb, c), lambda n, i, blk=blk: (blk(n, i), 0)))
        out_shape.append(jax.ShapeDtypeStruct((r, c), BF16))
        args.append(arr)
    kern = functools.partial(_proj_kernel, mode=mode, n_side=len(side), n_first_tiles=n_first_tiles,
                             first_scale=first_scale, rest_scale=rest_scale, normalize=normalize,
                             tn=tn, w_t=w_t)
    return pl.pallas_call(
        kern,
        grid=(nn, nm),
        in_specs=in_specs,
        out_specs=out_specs,
        out_shape=out_shape,
        scratch_shapes=scratch,
        compiler_params=_cparams(("arbitrary", "arbitrary")),
        name=name,
    )(*args)


def _ret_core_kernel(lg_ref, q_ref, k_ref, v_ref, gate_ref, gn_ref, o_ref,
                     state_ref, mask_ref, dq_ref, dk_ref, *, hps):
    t = pl.program_id(1)
    bt = q_ref.shape[0]

    @pl.when(t == 0)
    def _init():
        state_ref[...] = jnp.zeros_like(state_ref)
        r = lax.broadcasted_iota(jnp.int32, (bt, bt), 0)
        c = lax.broadcasted_iota(jnp.int32, (bt, bt), 1)
        dist = jnp.abs(r - c).astype(F32)
        visible = (c // CHUNK) <= (r // CHUNK)
        rr = lax.broadcasted_iota(jnp.int32, (bt, 1), 0).astype(F32)
        for hh in range(hps):
            lg = lg_ref[pl.program_id(0) * hps + hh]
            mask_ref[hh] = jnp.where(visible, jnp.exp(lg * dist), 0.0)
            dq_ref[hh] = jnp.exp(lg * (rr + 1.0))
            dk_ref[hh] = jnp.exp(lg * (bt - 1.0 - rr))

    for hh in range(hps):
        lg = lg_ref[pl.program_id(0) * hps + hh]
        q = q_ref[:, hh * RET_DK:(hh + 1) * RET_DK]
        k = k_ref[:, hh * RET_DK:(hh + 1) * RET_DK]
        v = v_ref[:, hh * RET_DV:(hh + 1) * RET_DV]
        s = lax.dot_general(q, k, (((1,), (1,)), ((), ())), preferred_element_type=F32)
        p = (s * mask_ref[hh]).astype(BF16)
        y = jnp.dot(p, v, preferred_element_type=F32)
        st = state_ref[hh]
        y = y + dq_ref[hh] * jnp.dot(q, st.astype(BF16), preferred_element_type=F32)
        kd = (k.astype(F32) * dk_ref[hh]).astype(BF16)
        upd = lax.dot_general(kd, v, (((0,), (0,)), ((), ())), preferred_element_type=F32)
        state_ref[hh] = st * jnp.exp(lg * bt) + upd

        mu = jnp.mean(y, axis=-1, keepdims=True)
        d = y - mu
        var = jnp.mean(d * d, axis=-1, keepdims=True)
        yn = d * lax.rsqrt(var + GN_EPS) * gn_ref[:, hh * RET_DV:(hh + 1) * RET_DV]
        gate = gate_ref[:, hh * RET_DV:(hh + 1) * RET_DV].astype(F32)
        o_ref[:, hh * RET_DV:(hh + 1) * RET_DV] = (_silu(gate) * yn).astype(o_ref.dtype)


def _ret_core(log_gamma, qk, vg, gn_g):
    s = qk.shape[0]
    bt = min(RET_BT, s)
    hps = RET_HPS
    nqb = RET_QK // (hps * RET_DK)
    nvb = RET_VW // (hps * RET_DV)
    grid_spec = pltpu.PrefetchScalarGridSpec(
        num_scalar_prefetch=1,
        grid=(RET_HEADS // hps, s // bt),
        in_specs=[pl.BlockSpec((bt, hps * RET_DK), lambda h, t, lg: (t, h)),
                  pl.BlockSpec((bt, hps * RET_DK), lambda h, t, lg: (t, nqb + h)),
                  pl.BlockSpec((bt, hps * RET_DV), lambda h, t, lg: (t, h)),
                  pl.BlockSpec((bt, hps * RET_DV), lambda h, t, lg: (t, nvb + h)),
                  pl.BlockSpec((1, hps * RET_DV), lambda h, t, lg: (0, h))],
        out_specs=pl.BlockSpec((bt, hps * RET_DV), lambda h, t, lg: (t, h)),
        scratch_shapes=[pltpu.VMEM((hps, RET_DK, RET_DV), F32),
                        pltpu.VMEM((hps, bt, bt), F32),
                        pltpu.VMEM((hps, bt, 1), F32),
                        pltpu.VMEM((hps, bt, 1), F32)],
    )
    return pl.pallas_call(
        functools.partial(_ret_core_kernel, hps=hps),
        grid_spec=grid_spec,
        out_shape=jax.ShapeDtypeStruct((s, RET_VW), BF16),
        compiler_params=_cparams(("parallel", "arbitrary")),
        name="ret_core",
    )(log_gamma, qk, qk, vg, vg, gn_g)


def _mm_ln_kernel(a_ref, w_ref, r_ref, g_ref, b_ref, of_ref, ob_ref):
    half = a_ref.shape[0] // 2
    for rows in (slice(0, half), slice(half, 2 * half)):
        acc = jnp.dot(a_ref[rows, :], w_ref[...], preferred_element_type=F32)
        out = _layer_norm(DN_ALPHA * r_ref[rows, :] + acc, g_ref[...], b_ref[...])
        of_ref[rows, :] = out
        ob_ref[rows, :] = out.astype(ob_ref.dtype)


def _out_proj_ln(a, w, resid, g, b, name):
    m, k = a.shape
    n = w.shape[1]
    tm = min(LN_TM, m)
    return pl.pallas_call(
        _mm_ln_kernel,
        grid=(m // tm,),
        in_specs=[pl.BlockSpec((tm, k), lambda i: (i, 0)),
                  pl.BlockSpec((k, n), lambda i: (0, 0), pipeline_mode=pl.Buffered(1)),
                  pl.BlockSpec((tm, n), lambda i: (i, 0)),
                  pl.BlockSpec((1, n), lambda i: (0, 0)),
                  pl.BlockSpec((1, n), lambda i: (0, 0))],
        out_specs=[pl.BlockSpec((tm, n), lambda i: (i, 0)),
                   pl.BlockSpec((tm, n), lambda i: (i, 0))],
        out_shape=[jax.ShapeDtypeStruct((m, n), F32), jax.ShapeDtypeStruct((m, n), BF16)],
        compiler_params=_cparams(("arbitrary",)),
        name=name,
    )(a, w, resid, g, b)


def _mlp_kernel(xb_ref, xf_ref, w1_ref, w2_ref, g_ref, b_ref, of_ref, ob_ref, acc_ref):
    f = pl.program_id(1)

    @pl.when(f == 0)
    def _first():
        acc_ref[...] = jnp.zeros_like(acc_ref)

    h = jnp.dot(xb_ref[...], w1_ref[...], preferred_element_type=F32)
    h = jnp.square(jnp.maximum(h, 0.0)).astype(BF16)
    acc_ref[...] += jnp.dot(h, w2_ref[...], preferred_element_type=F32)

    @pl.when(f == pl.num_programs(1) - 1)
    def _fin():
        out = _layer_norm(DN_ALPHA * xf_ref[...] + acc_ref[...], g_ref[...], b_ref[...])
        of_ref[...] = out
        ob_ref[...] = out.astype(ob_ref.dtype)


def _mlp_ln(xb, xf, w1, w2, g, b, name):
    m, d = xb.shape
    ff = w1.shape[1]
    tm, tf = min(MLP_TM, m), MLP_TF
    return pl.pallas_call(
        _mlp_kernel,
        grid=(m // tm, ff // tf),
        in_specs=[pl.BlockSpec((tm, d), lambda i, f: (i, 0)),
                  pl.BlockSpec((tm, d), lambda i, f: (i, 0)),
                  pl.BlockSpec((d, tf), lambda i, f: (0, f)),
                  pl.BlockSpec((tf, d), lambda i, f: (f, 0)),
                  pl.BlockSpec((1, d), lambda i, f: (0, 0)),
                  pl.BlockSpec((1, d), lambda i, f: (0, 0))],
        out_specs=[pl.BlockSpec((tm, d), lambda i, f: (i, 0)),
                   pl.BlockSpec((tm, d), lambda i, f: (i, 0))],
        out_shape=[jax.ShapeDtypeStruct((m, d), F32), jax.ShapeDtypeStruct((m, d), BF16)],
        scratch_shapes=[pltpu.VMEM((tm, d), F32)],
        compiler_params=_cparams(("parallel", "arbitrary")),
        name=name,
    )(xb, xf, w1, w2, g, b)


def _gdn_ba_kernel(x_ref, w_ref, alog_ref, dtb_ref, o_ref, ot_ref, *, gsz):
    nh = GDN_V_HEADS
    w = w_ref[...]
    parts = []
    for gi in range(nh // gsz):
        parts += [w[gi * gsz:(gi + 1) * gsz, :], w[nh + gi * gsz:nh + (gi + 1) * gsz, :]]
    wg = jnp.concatenate(parts, axis=0).astype(BF16)
    acc = lax.dot_general(x_ref[...].astype(BF16), wg, (((1,), (1,)), ((), ())),
                          preferred_element_type=F32)
    tm, nc = acc.shape
    acc = jnp.concatenate([acc, jnp.zeros((tm, V7X_LANES - nc), F32)], axis=1)
    lane = lax.broadcasted_iota(jnp.int32, acc.shape, 1)
    is_beta = (lane % (2 * gsz)) < gsz
    beta = 1.0 / (1.0 + jnp.exp(-acc))
    a = acc + dtb_ref[...]
    softplus = jnp.maximum(a, 0.0) + jnp.log(1.0 + jnp.exp(-jnp.abs(a)))
    g = -jnp.exp(alog_ref[...]) * softplus
    row = lax.broadcasted_iota(jnp.int32, acc.shape, 0) % CHUNK
    sh = 1
    while sh < CHUNK:
        g = g + jnp.where(row >= sh, pltpu.roll(g, sh, 0), 0.0)
        sh *= 2
    res = jnp.where(is_beta, beta, g)
    res_t = res.T
    for gi in range(nc // (2 * gsz)):
        o_ref[gi] = res[:, gi * 2 * gsz:(gi + 1) * 2 * gsz]
        ot_ref[gi] = res_t[gi * 2 * gsz:(gi + 1) * 2 * gsz, :]


def _gdn_beta_decay(xb, w_t, layer, row_off, a_log, dt_bias, gsz):
    m, k = xb.shape
    tm = min(GDN_BA_TM, m)
    nc = 2 * GDN_V_HEADS
    ng = GDN_V_HEADS // gsz
    kern = functools.partial(_gdn_ba_kernel, gsz=gsz)
    return pl.pallas_call(
        kern,
        grid=(m // tm,),
        in_specs=[pl.BlockSpec((tm, k), lambda i: (i, 0)),
                  pl.BlockSpec((None, nc, k), lambda i: (layer, row_off // nc, 0)),
                  pl.BlockSpec((1, V7X_LANES), lambda i: (0, 0)),
                  pl.BlockSpec((1, V7X_LANES), lambda i: (0, 0))],
        out_specs=[pl.BlockSpec((ng, tm, 2 * gsz), lambda i: (0, i, 0)),
                   pl.BlockSpec((ng, 2 * gsz, tm), lambda i: (0, 0, i))],
        out_shape=[jax.ShapeDtypeStruct((ng, m, 2 * gsz), F32),
                   jax.ShapeDtypeStruct((ng, 2 * gsz, m), F32)],
        compiler_params=_cparams(("parallel",)),
        name="gdn_beta_decay",
    )(xb, w_t, a_log, dt_bias)


def _pair_blockdiag(x2, left):
    top = jnp.where(left, x2, 0.0).astype(BF16)
    bot = jnp.where(left, 0.0, x2).astype(BF16)
    return jnp.concatenate([top, bot], axis=0)


def _gdn_core_kernel(q_ref, k_ref, v_ref, z_ref, bg_ref, bgt_ref, ng_ref, o_ref, state_ref,
                     *, kg, nc):
    t = pl.program_id(1)
    c = CHUNK
    gsz = 2 * kg
    dv = GDN_DV

    @pl.when(t == 0)
    def _init():
        state_ref[...] = jnp.zeros_like(state_ref)

    r = lax.broadcasted_iota(jnp.int32, (c, 2 * c), 0)
    lane = lax.broadcasted_iota(jnp.int32, (c, 2 * c), 1)
    left = lane < c
    col = lane & (c - 1)
    causal = r >= col
    strict = r > col
    eye2 = jnp.where(r == col, 1.0, 0.0).astype(F32)
    blk_mask = (r // GDN_BLK) == (col // GDN_BLK)
    ng = ng_ref[...]
    zeros_v = jnp.zeros((c, dv), BF16)

    bg_all = bg_ref[...]
    bgt_all = bgt_ref[...]

    pairs = [(ci, kh) for ci in range(nc) for kh in range(kg)]

    def pmm(x2, y2):
        return jnp.dot(x2.astype(BF16), _pair_blockdiag(y2, left), preferred_element_type=F32)

    qs, ks, a2s, attn2s, betas, egcs, ekls, cdecs = {}, {}, {}, {}, {}, {}, {}, {}
    for p in pairs:
        ci, kh = p
        rows = slice(ci * c, (ci + 1) * c)
        kcols = slice(kh * GDN_DK, (kh + 1) * GDN_DK)
        q = q_ref[rows, kcols]
        k = k_ref[rows, kcols]
        qs[p], ks[p] = q, k
        qk = jnp.concatenate([q, k], axis=0)
        kk2 = jnp.concatenate([k, k], axis=0)
        prod = lax.dot_general(qk, kk2, (((1,), (1,)), ((), ())), preferred_element_type=F32)
        qkt2, kkt2 = prod[:c], prod[c:]
        bcol, gcol, grow, glast = [], [], [], []
        for vi in range(2):
            hl = 2 * kh + vi
            bcol.append(jnp.broadcast_to(bg_all[rows, hl:hl + 1], (c, dv)))
            gcol.append(jnp.broadcast_to(bg_all[rows, gsz + hl:gsz + hl + 1], (c, dv)))
            grow.append(bgt_all[gsz + hl:gsz + hl + 1, rows])
            glast.append(gcol[vi][c - 1:c, :])
        beta2 = jnp.where(left, bcol[0], bcol[1])
        gc2 = jnp.where(left, gcol[0], gcol[1])
        grow2 = jnp.concatenate(grow, axis=1)
        dec2 = jnp.exp(jnp.where(causal, gc2 - grow2, -jnp.inf))
        a2s[p] = jnp.where(strict, beta2 * kkt2 * dec2, 0.0)
        attn2s[p] = qkt2 * dec2
        betas[p] = bcol
        egcs[p] = [jnp.exp(gcol[0]), jnp.exp(gcol[1])]
        ekls[p] = [jnp.exp(glast[0] - gcol[0]), jnp.exp(glast[1] - gcol[1])]
        cdecs[p] = jnp.concatenate([jnp.exp(glast[0]), jnp.exp(glast[1])], axis=1)

    ad = {p: jnp.where(blk_mask, a2s[p], 0.0) for p in pairs}
    ao = {p: a2s[p] - ad[p] for p in pairs}
    ima = {p: eye2 - ad[p] for p in pairs}
    x2 = {p: pmm(ad[p], ad[p]) for p in pairs}
    s1 = {p: pmm(jnp.concatenate([x2[p], ima[p]], axis=0), x2[p]) for p in pairs}
    x4 = {p: s1[p][:c] for p in pairs}
    p1 = {p: s1[p][c:] + ima[p] for p in pairs}
    s2 = {p: pmm(jnp.concatenate([x4[p], p1[p]], axis=0), x4[p]) for p in pairs}
    p2 = {p: s2[p][c:] + p1[p] for p in pairs}
    dinv = {p: pmm(p2[p], s2[p][:c]) + p2[p] for p in pairs}
    n1 = {p: pmm(dinv[p], ao[p]) for p in pairs}
    s3 = {p: jnp.dot(n1[p].astype(BF16),
                     jnp.concatenate([_pair_blockdiag(dinv[p], left),
                                      _pair_blockdiag(n1[p], left)], axis=1),
                     preferred_element_type=F32) for p in pairs}
    gm = {p: dinv[p] - s3[p][:, :2 * c] for p in pairs}
    tm = {p: gm[p] + pmm(s3[p][:, 2 * c:], gm[p]) for p in pairs}

    us, ws = {}, {}
    for p in pairs:
        ci, kh = p
        rows = slice(ci * c, (ci + 1) * c)
        kf = ks[p].astype(F32)
        tb = tm[p].astype(BF16)
        u2, w2 = [], []
        for vi in range(2):
            hl = 2 * kh + vi
            v = v_ref[rows, hl * dv:(hl + 1) * dv].astype(F32)
            beta = betas[p][vi]
            rhs = jnp.concatenate([v * beta, kf * (beta * egcs[p][vi])], axis=1).astype(BF16)
            uw = jnp.dot(tb[:, vi * c:(vi + 1) * c], rhs, preferred_element_type=F32)
            u2.append(uw[:, :dv])
            w2.append(uw[:, dv:])
        us[p] = jnp.concatenate(u2, axis=1)
        ws[p] = jnp.concatenate(w2, axis=1).astype(BF16)

    for ci in range(nc):
        rows = slice(ci * c, (ci + 1) * c)
        st = {kh: state_ref[kh] for kh in range(kg)}
        stb = {kh: st[kh].astype(BF16) for kh in range(kg)}
        zk = jnp.zeros((GDN_DK, dv), BF16)
        wst, qst = {}, {}
        for kh in range(kg):
            p = (ci, kh)
            sbd = jnp.concatenate([jnp.concatenate([stb[kh][:, :dv], zk], axis=1),
                                   jnp.concatenate([zk, stb[kh][:, dv:]], axis=1)], axis=0)
            wst[kh] = jnp.dot(ws[p], sbd, preferred_element_type=F32)
            qst[kh] = jnp.dot(qs[p], stb[kh], preferred_element_type=F32)
        for kh in range(kg):
            p = (ci, kh)
            vn = us[p] - wst[kh]
            vnb = vn.astype(BF16)
            vbd = jnp.concatenate([jnp.concatenate([vnb[:, :dv], zeros_v], axis=1),
                                   jnp.concatenate([zeros_v, vnb[:, dv:]], axis=1)], axis=0)
            av = jnp.dot(attn2s[p].astype(BF16), vbd, preferred_element_type=F32)
            ekl2 = jnp.concatenate(ekls[p], axis=1)
            svn = (vn * ekl2).astype(BF16)
            upd = lax.dot_general(ks[p], svn, (((0,), (0,)), ((), ())), preferred_element_type=F32)
            state_ref[kh] = st[kh] * cdecs[p] + upd
            egc2 = jnp.concatenate(egcs[p], axis=1)
            y2 = egc2 * qst[kh] + av
            for vi in range(2):
                hl = 2 * kh + vi
                y = y2[:, vi * dv:(vi + 1) * dv]
                ms = jnp.mean(y * y, axis=-1, keepdims=True)
                yn = y * lax.rsqrt(ms + RMS_EPS) * ng
                zz = z_ref[rows, hl * dv:(hl + 1) * dv].astype(F32)
                o_ref[rows, hl * dv:(hl + 1) * dv] = (yn * _silu(zz)).astype(o_ref.dtype)


def _gdn_core(qk, v, z, bg, bgt, norm_g):
    s = qk.shape[0]
    kg, nc = GDN_KG, GDN_NC
    bt = nc * CHUNK
    kern = functools.partial(_gdn_core_kernel, kg=kg, nc=nc)
    vw = 2 * kg * GDN_DV
    k_off = GDN_K_HEADS // kg
    return pl.pallas_call(
        kern,
        grid=(GDN_K_HEADS // kg, s // bt),
        in_specs=[pl.BlockSpec((bt, kg * GDN_DK), lambda g, t: (t, g)),
                  pl.BlockSpec((bt, kg * GDN_DK), lambda g, t: (t, g + k_off)),
                  pl.BlockSpec((bt, vw), lambda g, t: (t, g)),
                  pl.BlockSpec((bt, vw), lambda g, t: (t, g)),
                  pl.BlockSpec((None, bt, 4 * kg), lambda g, t: (g, t, 0)),
                  pl.BlockSpec((None, 4 * kg, bt), lambda g, t: (g, 0, t)),
                  pl.BlockSpec((1, GDN_DV), lambda g, t: (0, 0))],
        out_specs=pl.BlockSpec((bt, vw), lambda g, t: (t, g)),
        out_shape=jax.ShapeDtypeStruct((s, GDN_VW), BF16),
        scratch_shapes=[pltpu.VMEM((kg, GDN_DK, 2 * GDN_DV), F32)],
        compiler_params=_cparams(("parallel", "arbitrary")),
        name="gdn_core",
    )(qk, qk, v, z, bg, bgt, norm_g)


def _rotary_tables(s):
    half = RET_DK // 2
    inv = np.float32(ROPE_BASE) ** (-np.arange(half, dtype=np.float32) / np.float32(half))
    ang = np.arange(s, dtype=np.float32)[:, None] * inv[None, :]
    return jnp.asarray(np.cos(ang), F32), jnp.asarray(np.sin(ang), F32)


def _retention_mixer(xin, xf, w_in, gn_g, w_out, mlp_w1, mlp_w2, layer, j, ln_g, ln_b):
    s = xin.shape[0]
    tm = MM_TM if xin.dtype == BF16 else MM_TM_F32
    log_gamma = jnp.log1p(-jnp.exp2(-5.0 - jnp.arange(RET_HEADS, dtype=F32)))
    qk, w_out_b = _proj(xin, w_in, j, 0, 2 * RET_QK, "ret_qk_proj", tm=tm, mode="rotary",
                        rotary=_rotary_tables(s), n_first_tiles=RET_QK // MM_TN,
                        first_scale=1.0, rest_scale=RET_DK ** -0.5,
                        side=((w_out, j),))
    vg, w1b, w2b = _proj(xin, w_in, j, 2 * RET_QK, 2 * RET_VW, "ret_vg_proj", tm=tm,
                         side=((mlp_w1, layer), (mlp_w2, layer)))
    a = _ret_core(log_gamma, qk, vg, gn_g[j].reshape(1, RET_VW))
    xf, xb = _out_proj_ln(a, w_out_b, xf, ln_g.reshape(1, -1), ln_b.reshape(1, -1), "ret_out_ln")
    return xf, xb, w1b, w2b


def _gdn_mixer(xin, xf, w_in, conv_w, a_log, dt_bias, norm_g, w_out, mlp_w1, mlp_w2, layer, j,
               ln_g, ln_b):
    nh, gsz = GDN_V_HEADS, 2 * GDN_KG
    ng = nh // gsz
    tm = MM_TM if xin.dtype == BF16 else MM_TM_F32
    col0 = GDN_QKV + GDN_VW
    zeros = jnp.zeros((ng, gsz), F32)

    def _lanes(vec):
        tab = jnp.concatenate([zeros, vec.astype(F32).reshape(ng, gsz)], axis=1).reshape(1, 2 * nh)
        return jnp.pad(tab, ((0, 0), (0, V7X_LANES - 2 * nh)))

    w_t = jnp.swapaxes(w_in, 1, 2)
    qk, w_out_b = _proj(xin, w_t, j, 0, 2 * GDN_QK, "gdn_qk_proj", tm=tm, w_t=True, mode="conv",
                        conv_w=conv_w, conv_layer=j, n_first_tiles=GDN_QK // MM_TN,
                        first_scale=GDN_DK ** -0.5, rest_scale=1.0, normalize=True,
                        side=((w_out, j),))
    v, w1b = _proj(xin, w_t, j, 2 * GDN_QK, GDN_VW, "gdn_v_proj", tm=tm, w_t=True, mode="conv",
                   conv_w=conv_w, conv_layer=j, side=((mlp_w1, layer),))
    z, w2b = _proj(xin, w_t, j, GDN_QKV, GDN_VW, "gdn_z_proj", tm=tm, w_t=True,
                   side=((mlp_w2, layer),))
    bg, bgt = _gdn_beta_decay(xin, w_t, j, col0, _lanes(a_log[j]), _lanes(dt_bias[j]), gsz)
    a = _gdn_core(qk, v, z, bg, bgt, norm_g[j].reshape(1, GDN_DV))
    xf, xb = _out_proj_ln(a, w_out_b, xf, ln_g.reshape(1, -1), ln_b.reshape(1, -1), "gdn_out_ln")
    return xf, xb, w1b, w2b


def kernel(x, ret_w_in, ret_gn_g, ret_w_out, gdn_w_in, gdn_conv_w, gdn_a_log, gdn_dt_bias,
           gdn_norm_g, gdn_w_out, ln_mix_g, ln_mix_b, mlp_w1, mlp_w2, ln_ffn_g, ln_ffn_b):
    b, s, d = x.shape
    outs = []
    for bi in range(b):
        xf = x.reshape(b * s, d) if b == 1 else x[bi]
        xb = xf.astype(BF16)
        for i in range(DEPTH):
            j = i // 2
            if i % 2 == 0:
                xf, xb, w1b, w2b = _retention_mixer(xb, xf, ret_w_in, ret_gn_g, ret_w_out,
                                                    mlp_w1, mlp_w2, i, j, ln_mix_g[i], ln_mix_b[i])
            else:
                xf, xb, w1b, w2b = _gdn_mixer(xb, xf, gdn_w_in, gdn_conv_w, gdn_a_log, gdn_dt_bias,
                                              gdn_norm_g, gdn_w_out, mlp_w1, mlp_w2, i, j,
                                              ln_mix_g[i], ln_mix_b[i])
            xf, xb = _mlp_ln(xb, xf, w1b, w2b, ln_ffn_g[i].reshape(1, -1),
                             ln_ffn_b[i].reshape(1, -1), f"mlp_ln_{i}")
        outs.append(xf)
    return outs[0].reshape(1, s, d) if b == 1 else jnp.stack(outs, axis=0)
```
